```python
import jax, jax.numpy as jnp
from jax import lax
import numpy as np

D_MODEL = 1024
BATCH = 4
SEQ = 8192
DEPTH = 2

CHUNK = 64
Q_BLOCK = 128
PLE_DIM = 256
EPS = 1e-6
NEG_INF = -1e30

SGU_WIDTH = 512
SGU_GROUPS = 4
SGU_GROUP_DIM = SGU_WIDTH // SGU_GROUPS
SGU_BLOCK = 128
POOL_WINDOWS = (2, 4, 8, 16)
POOL_WIDTH = 512
POOL_GROUPS = len(POOL_WINDOWS)
POOL_GROUP_DIM = POOL_WIDTH // POOL_GROUPS
MLA_HEADS = 4
MLA_NOPE = 128
MLA_ROPE = 64
MLA_QK = MLA_NOPE + MLA_ROPE
MLA_V = 128
MLA_WIDTH = MLA_HEADS * MLA_V
MLA_Q_RANK = 256
MLA_KV_RANK = 128
ROPE_THETA = 10000.0
SB_HEADS = 4
SB_HEAD_DIM = 128
SB_WIDTH = SB_HEADS * SB_HEAD_DIM

N_BRANCH = 4
IN_SIZES = (SGU_WIDTH, SGU_WIDTH, SGU_WIDTH,
            POOL_WIDTH, POOL_WIDTH,
            MLA_Q_RANK, MLA_KV_RANK, MLA_ROPE, MLA_WIDTH,
            SB_WIDTH, SB_WIDTH, SB_WIDTH, SB_WIDTH,
            N_BRANCH * D_MODEL)
N_IN = sum(IN_SIZES)

kernel_name = 'hybrid_gated_parallel_streaming_block'


def rms_norm(x, g):
    xf = x.astype(jnp.float32)
    y = xf * lax.rsqrt(jnp.mean(xf * xf, axis=-1, keepdims=True) + EPS)
    return (y * g.astype(jnp.float32)).astype(x.dtype)


def layer_norm(x, g, b):
    xf = x.astype(jnp.float32)
    mu = jnp.mean(xf, axis=-1, keepdims=True)
    xc = xf - mu
    y = xc * lax.rsqrt(jnp.mean(xc * xc, axis=-1, keepdims=True) + EPS)
    return (y * g.astype(jnp.float32) + b.astype(jnp.float32)).astype(x.dtype)


def apply_rope(x, positions):
    half = x.shape[-1] // 2
    freqs = ROPE_THETA ** (-jnp.arange(half, dtype=jnp.float32) / half)
    ang = positions.astype(jnp.float32)[:, :, None] * freqs
    cos = jnp.cos(ang)[:, :, None, :]
    sin = jnp.sin(ang)[:, :, None, :]
    xf = x.astype(jnp.float32)
    x1, x2 = xf[..., :half], xf[..., half:]
    return jnp.concatenate([x1 * cos - x2 * sin, x2 * cos + x1 * sin], axis=-1).astype(x.dtype)


def sweep_query_blocks(block_fn, q):
    b, s, h, d = q.shape
    nb = s // Q_BLOCK
    qb = q.reshape(b, nb, Q_BLOCK, h, d).transpose(1, 0, 2, 3, 4)
    out = lax.map(block_fn, (qb, jnp.arange(nb)))
    return out.transpose(1, 0, 2, 3, 4).reshape(b, s, h, out.shape[-1])


def spatial_gating(u, v, ln_g, ln_b, w_s, b_s):
    b, s, _ = v.shape
    v = layer_norm(v, ln_g, ln_b)
    nb = s // SGU_BLOCK
    vb = v.reshape(b, nb, SGU_BLOCK, SGU_GROUPS, SGU_GROUP_DIM)
    pos_chunk = jnp.arange(SGU_BLOCK) // CHUNK
    mask = pos_chunk[None, :] <= pos_chunk[:, None]
    w = jnp.where(mask[None], w_s, jnp.zeros_like(w_s))
    mixed = jnp.einsum('gts,bnsgc->bntgc', w, vb) + b_s.T[None, None, :, :, None]
    return u * mixed.reshape(b, s, SGU_WIDTH)


def multiscale_pool(xb, w_pool, scale):
    b, s, _ = xb.shape
    xf = xb.astype(jnp.float32).reshape(b, s, POOL_GROUPS, POOL_GROUP_DIM)
    cs = jnp.cumsum(xf, axis=1)
    t = jnp.arange(s)
    pooled = []
    for g, win in enumerate(POOL_WINDOWS):
        csg = cs[:, :, g]
        lagged = jnp.pad(csg, ((0, 0), (win, 0), (0, 0)))[:, :s]
        count = jnp.minimum(t + 1, win).astype(jnp.float32)[None, :, None]
        pooled.append((csg - lagged) / count - xf[:, :, g])
    pooled = jnp.stack(pooled, axis=2)
    y = jnp.einsum('bsgc,gcd->bsgd', pooled, w_pool.astype(jnp.float32)).reshape(b, s, POOL_WIDTH)
    return (y * scale.astype(jnp.float32)).astype(xb.dtype)


def latent_attention(c_q, c_kv, c_kr, positions, cq_g, w_uq, ckv_g, w_ukv, qn_g, kn_g):
    b, s, _ = c_q.shape
    q = (rms_norm(c_q, cq_g) @ w_uq).reshape(b, s, MLA_HEADS, MLA_QK)
    kv = (rms_norm(c_kv, ckv_g) @ w_ukv).reshape(b, s, MLA_HEADS, MLA_NOPE + MLA_V)
    k_nope, v = kv[..., :MLA_NOPE], kv[..., MLA_NOPE:]
    k_rope = jnp.broadcast_to(c_kr[:, :, None, :], (b, s, MLA_HEADS, MLA_ROPE))
    k = jnp.concatenate([k_nope, k_rope], axis=-1)
    q = rms_norm(q, qn_g)
    k = rms_norm(k, kn_g)
    q = jnp.concatenate([q[..., :MLA_NOPE], apply_rope(q[..., MLA_NOPE:], positions)], axis=-1)
    k = jnp.concatenate([k[..., :MLA_NOPE], apply_rope(k[..., MLA_NOPE:], positions)], axis=-1)
    qf, kf, vf = q.astype(jnp.float32), k.astype(jnp.float32), v.astype(jnp.float32)
    key_chunk = jnp.arange(s) // CHUNK
    scale = MLA_QK ** -0.5

    def block(args):
        qi, i = args
        sc = jnp.einsum('bqhd,bkhd->bhqk', qi, kf) * scale
        q_chunk = (i * Q_BLOCK + jnp.arange(Q_BLOCK)) // CHUNK
        mask = key_chunk[None, :] <= q_chunk[:, None]
        w = jax.nn.softmax(jnp.where(mask, sc, NEG_INF), axis=-1)
        return jnp.einsum('bhqk,bkhd->bqhd', w, vf)

    o = sweep_query_blocks(block, qf)
    return o.reshape(b, s, MLA_WIDTH).astype(c_q.dtype)


def stick_breaking_attention(q, k, v):
    b, s, _ = q.shape
    qf = q.astype(jnp.float32).reshape(b, s, SB_HEADS, SB_HEAD_DIM)
    kf = k.astype(jnp.float32).reshape(b, s, SB_HEADS, SB_HEAD_DIM)
    vf = v.astype(jnp.float32).reshape(b, s, SB_HEADS, SB_HEAD_DIM)
    key_pos = jnp.arange(s)
    scale = SB_HEAD_DIM ** -0.5

    def block(args):
        qi, i = args
        z = jnp.einsum('bqhd,bkhd->bhqk', qi, kf) * scale
        q_pos = i * Q_BLOCK + jnp.arange(Q_BLOCK)
        strict = key_pos[None, :] < q_pos[:, None]
        log_keep = jnp.where(strict, jax.nn.log_sigmoid(-z), 0.0)
        after = lax.cumsum(log_keep, axis=3, reverse=True) - log_keep
        a = jnp.where(strict, jnp.exp(jax.nn.log_sigmoid(z) + after), 0.0)
        return jnp.einsum('bhqk,bkhd->bqhd', a, vf)

    o = sweep_query_blocks(block, qf)
    return o.reshape(b, s, SB_WIDTH).astype(q.dtype)


def setup_inputs(seed: int = 0) -> dict:
    key = jax.random.key(seed)
    ks = jax.random.split(key, 32)
    f32 = jnp.float32

    def nrm(k, shape, scale):
        return jax.random.normal(k, shape, f32) * scale

    def gain(k, shape):
        return 1.0 + 0.05 * jax.random.normal(k, shape, f32)

    x = jax.random.normal(ks[0], (BATCH, SEQ, D_MODEL), f32)
    p = jax.random.normal(ks[1], (DEPTH, BATCH, SEQ, PLE_DIM), f32)
    offsets = jax.random.randint(ks[2], (BATCH, 1), 0, 4096, dtype=jnp.int32)
    positions = (jnp.arange(SEQ, dtype=jnp.int32)[None, :] + offsets).astype(jnp.int32)
    return {
        'x': x,
        'p': p,
        'positions': positions,
        'norm_g': gain(ks[3], (DEPTH, D_MODEL)),
        'w_in': nrm(ks[4], (DEPTH, D_MODEL, N_IN), D_MODEL ** -0.5),
        'ln_v_g': gain(ks[5], (DEPTH, SGU_WIDTH)),
        'ln_v_b': nrm(ks[6], (DEPTH, SGU_WIDTH), 0.02),
        'sgu_w': nrm(ks[7], (DEPTH, SGU_GROUPS, SGU_BLOCK, SGU_BLOCK), SGU_BLOCK ** -0.5),
        'sgu_b': 1.0 + 0.1 * jax.random.normal(ks[8], (DEPTH, SGU_GROUPS, SGU_BLOCK), f32),
        'w_a_out': nrm(ks[9], (DEPTH, SGU_WIDTH, D_MODEL), SGU_WIDTH ** -0.5),
        'pool_w': nrm(ks[10], (DEPTH, POOL_GROUPS, POOL_GROUP_DIM, POOL_GROUP_DIM), POOL_GROUP_DIM ** -0.5),
        'pool_scale': 1.0 + 0.1 * jax.random.normal(ks[11], (DEPTH, POOL_WIDTH), f32),
        'w_b_out': nrm(ks[12], (DEPTH, POOL_WIDTH, D_MODEL), POOL_WIDTH ** -0.5),
        'cq_norm_g': gain(ks[13], (DEPTH, MLA_Q_RANK)),
        'w_uq': nrm(ks[14], (DEPTH, MLA_Q_RANK, MLA_HEADS * MLA_QK), MLA_Q_RANK ** -0.5),
        'ckv_norm_g': gain(ks[15], (DEPTH, MLA_KV_RANK)),
        'w_ukv': nrm(ks[16], (DEPTH, MLA_KV_RANK, MLA_HEADS * (MLA_NOPE + MLA_V)), MLA_KV_RANK ** -0.5),
        'q_norm_g': gain(ks[17], (DEPTH, MLA_QK)),
        'k_norm_g': gain(ks[18], (DEPTH, MLA_QK)),
        'w_c_out': nrm(ks[19], (DEPTH, MLA_WIDTH, D_MODEL), MLA_WIDTH ** -0.5),
        'w_d_out': nrm(ks[20], (DEPTH, SB_WIDTH, D_MODEL), SB_WIDTH ** -0.5),
        'w_o': nrm(ks[21], (DEPTH, D_MODEL, D_MODEL), D_MODEL ** -0.5),
        'w_ple': nrm(ks[22], (DEPTH, PLE_DIM, D_MODEL), PLE_DIM ** -0.5),
        'ple_norm_g': gain(ks[23], (DEPTH, D_MODEL)),
        'w_ple_gate': nrm(ks[24], (DEPTH, D_MODEL, D_MODEL), D_MODEL ** -0.5),
    }


def reference(x, p, positions, norm_g, w_in, ln_v_g, ln_v_b, sgu_w, sgu_b, w_a_out,
              pool_w, pool_scale, w_b_out, cq_norm_g, w_uq, ckv_norm_g, w_ukv,
              q_norm_g, k_norm_g, w_c_out, w_d_out, w_o, w_ple, ple_norm_g, w_ple_gate):
    b, s, d = x.shape
    split_at = [int(o) for o in np.cumsum(IN_SIZES)[:-1]]
    for i in range(DEPTH):
        h = rms_norm(x, norm_g[i])
        proj = h @ w_in[i]
        (a_u, a_v, a_z, b_x, b_z, c_q, c_kv, c_kr, c_z,
         d_q, d_k, d_v, d_z, gates) = jnp.split(proj, split_at, axis=-1)

        y_a = (spatial_gating(a_u, a_v, ln_v_g[i], ln_v_b[i], sgu_w[i], sgu_b[i])
               * jax.nn.silu(a_z)) @ w_a_out[i]
        y_b = (multiscale_pool(b_x, pool_w[i], pool_scale[i]) * jax.nn.silu(b_z)) @ w_b_out[i]
        y_c = (latent_attention(c_q, c_kv, c_kr, positions, cq_norm_g[i], w_uq[i],
                                ckv_norm_g[i], w_ukv[i], q_norm_g[i], k_norm_g[i])
               * jax.nn.silu(c_z)) @ w_c_out[i]
        y_d = (stick_breaking_attention(d_q, d_k, d_v) * jax.nn.silu(d_z)) @ w_d_out[i]

        g = jax.nn.sigmoid(gates).reshape(b, s, N_BRANCH, d)
        branches = jnp.stack([y_a, y_b, y_c, y_d], axis=2)
        merged = jnp.sum(g * branches, axis=2)
        x = x + merged @ w_o[i]

        e = p[i] @ w_ple[i]
        x = x + jax.nn.sigmoid(rms_norm(x, ple_norm_g[i]) @ w_ple_gate[i]) * e
    return x
```

```python
import functools

import numpy as np
import jax
import jax.numpy as jnp
from jax import lax
from jax.experimental import pallas as pl
from jax.experimental.pallas import tpu as pltpu

F32 = jnp.float32
BF16 = jnp.bfloat16

EPS = 1e-6
NEG_INF = -1e30
CHUNK = 64
D_MODEL = 1024
PLE_DIM = 256
SGU_BLOCK = 128
GROUP_DIM = 128
N_GROUPS = 4
BRANCH_WIDTH = 512
POOL_WINDOWS = (2, 4, 8, 16)
POOL_HALO = 16
HEADS = 4
MLA_NOPE = 128
MLA_ROPE = 64
MLA_QK = MLA_NOPE + MLA_ROPE
MLA_Q_RANK = 256
MLA_KV_RANK = 128
MLA_QK_PAD = 256
HEAD_DIM = 128
ROPE_THETA = 10000.0

LANES = 128
VMEM_LIMIT = 56 * 1024 * 1024

OFF_GATES = 0
OFF_AU = 4096
OFF_AV = OFF_AU + 512
OFF_AZ = OFF_AV + 512
OFF_BX = OFF_AZ + 512
OFF_BZ = OFF_BX + 512
OFF_CZ = OFF_BZ + 512
OFF_DZ = OFF_CZ + 512
OFF_DQ = OFF_DZ + 512
OFF_DK = OFF_DQ + 512
OFF_DV = OFF_DK + 512
OFF_CQ = OFF_DV + 512
OFF_CKV = OFF_CQ + MLA_Q_RANK
OFF_CKR = OFF_CKV + MLA_KV_RANK
N_PROJ = OFF_CKR + 2 * MLA_ROPE

SB_UNDERFLOW = 105.0


def _proj_column_order():
    sizes = (512, 512, 512, 512, 512, MLA_Q_RANK, MLA_KV_RANK, MLA_ROPE, 512, 512, 512, 512, 512, 4096)
    names = ("au", "av", "az", "bx", "bz", "cq", "ckv", "ckr", "cz", "dq", "dk", "dv", "dz", "gates")
    starts = dict(zip(names, np.cumsum((0,) + sizes[:-1])))
    size = dict(zip(names, sizes))
    order = ("gates", "au", "av", "az", "bx", "bz", "cz", "dz", "dq", "dk", "dv", "cq", "ckv", "ckr", "ckr")
    return np.concatenate([np.arange(starts[n], starts[n] + size[n]) for n in order])


def _uq_column_order():
    nope = [np.arange(h * MLA_QK, h * MLA_QK + MLA_NOPE) for h in range(HEADS)]
    rope = [np.arange(h * MLA_QK + MLA_NOPE, (h + 1) * MLA_QK) for h in range(HEADS)]
    return np.concatenate(nope + rope)


def _params(semantics):
    return pltpu.CompilerParams(dimension_semantics=semantics, vmem_limit_bytes=VMEM_LIMIT)


def _chunks(total, width):
    out, start = [], 0
    while start < total:
        size = min(width, total - start)
        out.append((start, size))
        start += size
    return out


def _inproj_kernel(x_ref, g_ref, w_ref, o_ref, h_sc):
    @pl.when(pl.program_id(1) == 0)
    def _():
        x = x_ref[...]
        ms = jnp.mean(x * x, axis=-1, keepdims=True)
        h_sc[...] = (x * lax.rsqrt(ms + EPS) * g_ref[...]).astype(BF16)

    h = h_sc[...]
    for start, size in _chunks(o_ref.shape[1], 512):
        o_ref[:, start:start + size] = jnp.dot(
            h, w_ref[:, start:start + size], preferred_element_type=F32).astype(BF16)


def _inproj(x2d, g, w):
    t, d = x2d.shape
    tm, tn = 1024, N_PROJ // 4
    return pl.pallas_call(
        _inproj_kernel,
        grid=(t // tm, N_PROJ // tn),
        in_specs=[
            pl.BlockSpec((tm, d), lambda i, j: (i, 0)),
            pl.BlockSpec((1, d), lambda i, j: (0, 0)),
            pl.BlockSpec((d, tn), lambda i, j: (0, j)),
        ],
        out_specs=pl.BlockSpec((tm, tn), lambda i, j: (i, j)),
        out_shape=jax.ShapeDtypeStruct((t, N_PROJ), BF16),
        scratch_shapes=[pltpu.VMEM((tm, d), BF16)],
        compiler_params=_params(("parallel", "arbitrary")),
        name="inproj",
    )(x2d, g, w)


def _rope_table_kernel(pos_ref, freq_ref, cos_ref, sin_ref):
    ang = freq_ref[...] * pos_ref[...].astype(F32)
    cos_ref[...] = jnp.cos(ang)
    sin_ref[...] = jnp.sin(ang)


def _rope_tables(positions):
    t = positions.size
    half = MLA_ROPE // 2
    tt = 2048
    freqs = (ROPE_THETA ** (-jnp.arange(half, dtype=F32) / half)).reshape(half, 1)
    cos_t, sin_t = pl.pallas_call(
        _rope_table_kernel,
        grid=(t // tt,),
        in_specs=[pl.BlockSpec((1, tt), lambda i: (0, i)),
                  pl.BlockSpec((half, 1), lambda i: (0, 0))],
        out_specs=[pl.BlockSpec((half, tt), lambda i: (0, i)),
                   pl.BlockSpec((half, tt), lambda i: (0, i))],
        out_shape=[jax.ShapeDtypeStruct((half, t), F32)] * 2,
        compiler_params=_params(("parallel",)),
        name="rope_tables",
    )(positions.reshape(1, t), freqs)
    sign = np.where((np.arange(LANES) % MLA_ROPE) < half, -1.0, 1.0).astype(np.float32)
    cos_l = jnp.tile(cos_t.T, (1, LANES // half))
    sin_l = jnp.tile(sin_t.T, (1, LANES // half)) * sign
    return cos_l, sin_l


def _mla_prep_kernel(cq_ref, ckv_ref, ckr_ref, cos_ref, sin_ref, cqg_ref, wuq_ref, ckvg_ref,
                     wukv_ref, qng_ref, kng_ref, q_ref, k_ref, v_ref):
    cos = cos_ref[...]
    sin = sin_ref[...]
    lane = lax.broadcasted_iota(jnp.int32, (1, LANES), 1)
    first_half = (lane % MLA_ROPE) < (MLA_ROPE // 2)
    low_group = lane < MLA_ROPE
    inv_qk = 1.0 / MLA_QK
    scale = MLA_QK ** -0.5

    def rotary(xr):
        partner = jnp.where(first_half, pltpu.roll(xr, LANES - MLA_ROPE // 2, 1),
                            pltpu.roll(xr, MLA_ROPE // 2, 1))
        return xr * cos + partner * sin

    def rms(xf, g):
        return xf * lax.rsqrt(jnp.mean(xf * xf, axis=-1, keepdims=True) + EPS) * g

    qng = qng_ref[...]
    kng = kng_ref[...]
    q_raw = jnp.dot(rms(cq_ref[...].astype(F32), cqg_ref[...]).astype(BF16), wuq_ref[...],
                    preferred_element_type=F32)
    kv_raw = jnp.dot(rms(ckv_ref[...].astype(F32), ckvg_ref[...]).astype(BF16), wukv_ref[...],
                     preferred_element_type=F32)
    ckr = ckr_ref[...].astype(F32)

    ss_kr = 0.5 * jnp.sum(ckr * ckr, axis=-1, keepdims=True)
    kr_rot = rotary(ckr * kng[:, MLA_NOPE:])
    for h in range(HEADS):
        kn = kv_raw[:, 2 * h * HEAD_DIM:(2 * h + 1) * HEAD_DIM]
        r = lax.rsqrt((jnp.sum(kn * kn, axis=-1, keepdims=True) + ss_kr) * inv_qk + EPS)
        mine = low_group if h % 2 == 0 else jnp.logical_not(low_group)
        k_ref[h, :, 0:MLA_NOPE] = (kn * r * kng[:, :MLA_NOPE]).astype(BF16)
        k_ref[h, :, MLA_NOPE:] = jnp.where(mine, kr_rot * r, 0.0).astype(BF16)
        v_ref[h] = kv_raw[:, (2 * h + 1) * HEAD_DIM:(2 * h + 2) * HEAD_DIM].astype(BF16)

    for j in range(HEADS // 2):
        blk = q_raw[:, HEADS * MLA_NOPE + j * LANES:HEADS * MLA_NOPE + (j + 1) * LANES]
        sq = blk * blk
        rs = []
        for h in (2 * j, 2 * j + 1):
            mine = low_group if h % 2 == 0 else jnp.logical_not(low_group)
            qn = q_raw[:, h * MLA_NOPE:(h + 1) * MLA_NOPE]
            ss = jnp.sum(qn * qn, axis=-1, keepdims=True) + jnp.sum(
                jnp.where(mine, sq, 0.0), axis=-1, keepdims=True)
            r = lax.rsqrt(ss * inv_qk + EPS)
            rs.append(r)
            q_ref[h, :, 0:MLA_NOPE] = (qn * r * qng[:, :MLA_NOPE] * scale).astype(BF16)
        rot = rotary(blk * jnp.where(low_group, rs[0], rs[1]) * qng[:, MLA_NOPE:]) * scale
        q_ref[2 * j, :, MLA_NOPE:] = jnp.where(low_group, rot, 0.0).astype(BF16)
        q_ref[2 * j + 1, :, MLA_NOPE:] = jnp.where(low_group, 0.0, rot).astype(BF16)


def _mla_prep(proj, cos_l, sin_l, cqg, wuq, ckvg, wukv, qng, kng):
    t = proj.shape[0]
    ts = 512
    const = lambda shape: pl.BlockSpec(shape, lambda i: (0,) * len(shape))
    return pl.pallas_call(
        _mla_prep_kernel,
        grid=(t // ts,),
        in_specs=[
            pl.BlockSpec((ts, MLA_Q_RANK), lambda i: (i, OFF_CQ // MLA_Q_RANK)),
            pl.BlockSpec((ts, MLA_KV_RANK), lambda i: (i, OFF_CKV // MLA_KV_RANK)),
            pl.BlockSpec((ts, LANES), lambda i: (i, OFF_CKR // LANES)),
            pl.BlockSpec((ts, LANES), lambda i: (i, 0)),
            pl.BlockSpec((ts, LANES), lambda i: (i, 0)),
            const((1, MLA_Q_RANK)), const(wuq.shape), const((1, MLA_KV_RANK)), const(wukv.shape),
            const((1, MLA_QK_PAD)), const((1, MLA_QK_PAD)),
        ],
        out_specs=[
            pl.BlockSpec((HEADS, ts, MLA_QK_PAD), lambda i: (0, i, 0)),
            pl.BlockSpec((HEADS, ts, MLA_QK_PAD), lambda i: (0, i, 0)),
            pl.BlockSpec((HEADS, ts, HEAD_DIM), lambda i: (0, i, 0)),
        ],
        out_shape=[
            jax.ShapeDtypeStruct((HEADS, t, MLA_QK_PAD), BF16),
            jax.ShapeDtypeStruct((HEADS, t, MLA_QK_PAD), BF16),
            jax.ShapeDtypeStruct((HEADS, t, HEAD_DIM), BF16),
        ],
        compiler_params=_params(("parallel",)),
        name="mla_prep",
    )(proj, proj, proj, cos_l, sin_l, cqg, wuq, ckvg, wukv, qng, kng)


def _mla_attn_kernel(q_ref, k_ref, v_ref, o_ref, *, tq):
    i = pl.program_id(2)
    q = q_ref[...]

    def step(j, carry, masked):
        m, l, acc = carry
        kj = k_ref[pl.ds(pl.multiple_of(j * tq, tq), tq), :]
        vj = v_ref[pl.ds(pl.multiple_of(j * tq, tq), tq), :]
        s = lax.dot_general(q, kj, (((1,), (1,)), ((), ())), preferred_element_type=F32)
        if masked:
            qc = lax.broadcasted_iota(jnp.int32, s.shape, 0) // CHUNK
            kc = lax.broadcasted_iota(jnp.int32, s.shape, 1) // CHUNK
            s = jnp.where(kc <= qc, s, NEG_INF)
        m_new = jnp.maximum(m, jnp.max(s, axis=-1, keepdims=True))
        alpha = jnp.exp(m - m_new)
        p = jnp.exp(s - m_new)
        l = alpha * l + jnp.sum(p, axis=-1, keepdims=True)
        acc = alpha * acc + jnp.dot(p.astype(BF16), vj, preferred_element_type=F32)
        return m_new, l, acc

    init = (jnp.full((tq, 1), -jnp.inf, F32), jnp.zeros((tq, 1), F32), jnp.zeros((tq, HEAD_DIM), F32))
    carry = lax.fori_loop(0, i, lambda j, c: step(j, c, False), init)
    _, l, acc = step(i, carry, True)
    o_ref[...] = (acc / l).astype(BF16)


def _mla_attn(q, k, v, batch, seq):
    tq = 512
    nq = seq // tq
    return pl.pallas_call(
        functools.partial(_mla_attn_kernel, tq=tq),
        grid=(batch, HEADS, nq),
        in_specs=[
            pl.BlockSpec((None, tq, MLA_QK_PAD), lambda b, h, i: (h, b * nq + i, 0)),
            pl.BlockSpec((None, seq, MLA_QK_PAD), lambda b, h, i: (h, b, 0)),
            pl.BlockSpec((None, seq, HEAD_DIM), lambda b, h, i: (h, b, 0)),
        ],
        out_specs=pl.BlockSpec((tq, HEAD_DIM), lambda b, h, i: (b * nq + i, h)),
        out_shape=jax.ShapeDtypeStruct((batch * seq, BRANCH_WIDTH), BF16),
        compiler_params=_params(("parallel", "parallel", "arbitrary")),
        name="mla_attn",
    )(q, k, v)


def _sb_attn_kernel(q_ref, k_ref, v_ref, o_ref, acc_sc, run_sc, *, tq):
    i = pl.program_id(2)
    q = q_ref[...]
    scale = HEAD_DIM ** -0.5
    row = lax.broadcasted_iota(jnp.int32, (tq, tq), 0)
    col = lax.broadcasted_iota(jnp.int32, (tq, tq), 1)
    tri = jnp.where(row >= col, 1.0, 0.0).astype(BF16)
    strict = col < row

    def tile(j, masked):
        kj = k_ref[pl.ds(pl.multiple_of(j * tq, tq), tq), :]
        vj = v_ref[pl.ds(pl.multiple_of(j * tq, tq), tq), :]
        z = lax.dot_general(q, kj, (((1,), (1,)), ((), ())), preferred_element_type=F32) * scale
        nz = -z
        log_keep = jnp.minimum(nz, 0.0) - jnp.log(1.0 + jnp.exp(jnp.minimum(z, nz)))
        if masked:
            log_keep = jnp.where(strict, log_keep, 0.0)
        hi = log_keep.astype(BF16)
        lo = (log_keep - hi.astype(F32)).astype(BF16)
        c = (jnp.dot(hi, tri, preferred_element_type=F32)
             + jnp.dot(lo, tri, preferred_element_type=F32))
        run = run_sc[...]
        a = jnp.exp(z + c + run)
        if masked:
            a = jnp.where(strict, a, 0.0)
        acc_sc[...] += jnp.dot(a.astype(BF16), vj, preferred_element_type=F32)
        run = run + c[:, 0:1]
        run_sc[...] = run
        return jnp.max(run)

    acc_sc[...] = jnp.zeros_like(acc_sc)
    run_sc[...] = jnp.zeros_like(run_sc)
    top = tile(i, True)

    def cond(state):
        j, top = state
        return jnp.logical_and(j >= 0, top > -SB_UNDERFLOW)

    def body(state):
        j, _ = state
        return j - 1, tile(j, False)

    lax.while_loop(cond, body, (i - 1, top))
    o_ref[...] = acc_sc[...].astype(BF16)


def _sb_attn(proj, batch, seq):
    tq = 256
    nq = seq // tq
    return pl.pallas_call(
        functools.partial(_sb_attn_kernel, tq=tq),
        grid=(batch, HEADS, nq),
        in_specs=[
            pl.BlockSpec((tq, HEAD_DIM), lambda b, h, i: (b * nq + i, OFF_DQ // HEAD_DIM + h)),
            pl.BlockSpec((seq, HEAD_DIM), lambda b, h, i: (b, OFF_DK // HEAD_DIM + h)),
            pl.BlockSpec((seq, HEAD_DIM), lambda b, h, i: (b, OFF_DV // HEAD_DIM + h)),
        ],
        out_specs=pl.BlockSpec((tq, HEAD_DIM), lambda b, h, i: (b * nq + i, h)),
        out_shape=jax.ShapeDtypeStruct((batch * seq, BRANCH_WIDTH), BF16),
        scratch_shapes=[pltpu.VMEM((tq, HEAD_DIM), F32), pltpu.VMEM((tq, 1), F32)],
        compiler_params=_params(("parallel", "parallel", "arbitrary")),
        name="sb_attn",
    )(proj, proj, proj)


def _silu(z):
    return z * jax.nn.sigmoid(z)


def _local_kernel(au_ref, av_ref, az_ref, bx_ref, halo_ref, bz_ref, lng_ref, lnb_ref, sw_ref, sb_ref,
                  pw_ref, ps_ref, ya_ref, yb_ref, *, tm, seq):
    i = pl.program_id(0)

    v = av_ref[...].astype(F32)
    mu = jnp.mean(v, axis=-1, keepdims=True)
    vc = v - mu
    vn = (vc * lax.rsqrt(jnp.mean(vc * vc, axis=-1, keepdims=True) + EPS) * lng_ref[...]
          + lnb_ref[...]).astype(BF16)
    tpos = lax.broadcasted_iota(jnp.int32, (SGU_BLOCK, SGU_BLOCK), 0) // CHUNK
    spos = lax.broadcasted_iota(jnp.int32, (SGU_BLOCK, SGU_BLOCK), 1) // CHUNK
    bias_t = sb_ref[...]
    for g in range(N_GROUPS):
        w = jnp.where(spos <= tpos, sw_ref[g], 0.0).astype(BF16)
        cols = slice(g * GROUP_DIM, (g + 1) * GROUP_DIM)
        for n in range(tm // SGU_BLOCK):
            rows = slice(n * SGU_BLOCK, (n + 1) * SGU_BLOCK)
            mixed = jnp.dot(w, vn[rows, cols], preferred_element_type=F32) + bias_t[:, g:g + 1]
            ya_ref[rows, cols] = (au_ref[rows, cols].astype(F32) * mixed
                                  * _silu(az_ref[rows, cols].astype(F32))).astype(BF16)

    start = (i * tm) % seq
    halo = jnp.where(start == 0, 0.0, halo_ref[...].astype(F32))
    xfull = jnp.concatenate([halo, bx_ref[...].astype(F32)], axis=0)
    pos = start + lax.broadcasted_iota(jnp.int32, (tm, 1), 0)
    for g, win in enumerate(POOL_WINDOWS):
        cols = slice(g * GROUP_DIM, (g + 1) * GROUP_DIM)
        xg = xfull[:, cols]
        s, shift = xg, 1
        while shift < win:
            s = s + pltpu.roll(s, shift, 0)
            shift *= 2
        count = jnp.minimum(pos + 1, win).astype(F32)
        pooled = s[POOL_HALO:] / count - xg[POOL_HALO:]
        y = jnp.dot(pooled.astype(BF16), pw_ref[g], preferred_element_type=F32) * ps_ref[:, cols]
        yb_ref[:, cols] = (y * _silu(bz_ref[:, cols].astype(F32))).astype(BF16)


def _local_mixers(proj, seq, lng, lnb, sgu_w, sgu_bt, pool_w, pool_scale):
    t = proj.shape[0]
    tm = 512
    w = BRANCH_WIDTH
    const = lambda shape: pl.BlockSpec(shape, lambda i: (0,) * len(shape))
    col = lambda off: pl.BlockSpec((tm, w), lambda i: (i, off // w))
    halo = pl.BlockSpec((POOL_HALO, w),
                        lambda i: (jnp.maximum(i * (tm // POOL_HALO) - 1, 0), OFF_BX // w))
    return pl.pallas_call(
        functools.partial(_local_kernel, tm=tm, seq=seq),
        grid=(t // tm,),
        in_specs=[col(OFF_AU), col(OFF_AV), col(OFF_AZ), col(OFF_BX), halo, col(OFF_BZ),
                  const((1, w)), const((1, w)), const(sgu_w.shape), const(sgu_bt.shape),
                  const(pool_w.shape), const((1, w))],
        out_specs=[pl.BlockSpec((tm, w), lambda i: (i, 0))] * 2,
        out_shape=[jax.ShapeDtypeStruct((t, w), BF16)] * 2,
        compiler_params=_params(("parallel",)),
        name="local_mixers",
    )(proj, proj, proj, proj, proj, proj, lng, lnb, sgu_w, sgu_bt, pool_w, pool_scale)


def _merge_kernel(ya_ref, yb_ref, oc_ref, od_ref, cz_ref, dz_ref, gates_ref, x_ref, p_ref,
                  wa_ref, wb_ref, wc_ref, wd_ref, wo_ref, wple_ref, pleg_ref, wpg_ref, o_ref):
    yc = (oc_ref[...].astype(F32) * _silu(cz_ref[...].astype(F32))).astype(BF16)
    yd = (od_ref[...].astype(F32) * _silu(dz_ref[...].astype(F32))).astype(BF16)
    merged = None
    for n, (y, w_ref) in enumerate(((ya_ref[...], wa_ref), (yb_ref[...], wb_ref),
                                    (yc, wc_ref), (yd, wd_ref))):
        gate = jax.nn.sigmoid(gates_ref[:, n * D_MODEL:(n + 1) * D_MODEL].astype(F32))
        term = gate * jnp.dot(y, w_ref[...], preferred_element_type=F32)
        merged = term if merged is None else merged + term
    x1 = x_ref[...] + jnp.dot(merged.astype(BF16), wo_ref[...], preferred_element_type=F32)
    e = jnp.dot(p_ref[...].astype(BF16), wple_ref[...], preferred_element_type=F32)
    n1 = (x1 * lax.rsqrt(jnp.mean(x1 * x1, axis=-1, keepdims=True) + EPS) * pleg_ref[...]).astype(BF16)
    o_ref[...] = x1 + jax.nn.sigmoid(jnp.dot(n1, wpg_ref[...], preferred_element_type=F32)) * e


def _merge(ya, yb, oc, od, proj, x2d, p2d, wa, wb, wc, wd, wo, wple, pleg, wpg):
    t, d = x2d.shape
    tm = 512
    w = BRANCH_WIDTH
    const = lambda shape: pl.BlockSpec(shape, lambda i: (0,) * len(shape))
    row = lambda width: pl.BlockSpec((tm, width), lambda i: (i, 0))
    return pl.pallas_call(
        _merge_kernel,
        grid=(t // tm,),
        in_specs=[row(w), row(w), row(w), row(w),
                  pl.BlockSpec((tm, w), lambda i: (i, OFF_CZ // w)),
                  pl.BlockSpec((tm, w), lambda i: (i, OFF_DZ // w)),
                  pl.BlockSpec((tm, 4 * d), lambda i: (i, OFF_GATES // (4 * d))),
                  row(d), row(PLE_DIM),
                  const(wa.shape), const(wb.shape), const(wc.shape), const(wd.shape),
                  const(wo.shape), const(wple.shape), const((1, d)), const(wpg.shape)],
        out_specs=row(d),
        out_shape=jax.ShapeDtypeStruct((t, d), F32),
        compiler_params=_params(("parallel",)),
        name="merge_out",
    )(ya, yb, oc, od, proj, proj, proj, x2d, p2d, wa, wb, wc, wd, wo, wple, pleg, wpg)


def kernel(x, p, positions, norm_g, w_in, ln_v_g, ln_v_b, sgu_w, sgu_b, w_a_out, pool_w, pool_scale,
           w_b_out, cq_norm_g, w_uq, ckv_norm_g, w_ukv, q_norm_g, k_norm_g, w_c_out, w_d_out, w_o,
           w_ple, ple_norm_g, w_ple_gate):
    batch, seq, d = x.shape
    depth = w_in.shape[0]
    t = batch * seq
    x2d = x.reshape(t, d)
    proj_order = _proj_column_order()
    uq_order = _uq_column_order()
    cos_l, sin_l = _rope_tables(positions)

    def qk_gain(g):
        rope = g[MLA_NOPE:]
        return jnp.concatenate([g[:MLA_NOPE], rope, rope]).reshape(1, MLA_QK_PAD)

    for i in range(depth):
        w_in_i = w_in[i][:, proj_order].astype(BF16)
        proj = _inproj(x2d, norm_g[i].reshape(1, d), w_in_i)
        q, k, v = _mla_prep(
            proj, cos_l, sin_l, cq_norm_g[i].reshape(1, -1), w_uq[i][:, uq_order].astype(BF16),
            ckv_norm_g[i].reshape(1, -1), w_ukv[i].astype(BF16), qk_gain(q_norm_g[i]), qk_gain(k_norm_g[i]))
        oc = _mla_attn(q, k, v, batch, seq)
        od = _sb_attn(proj, batch, seq)
        ya, yb = _local_mixers(
            proj, seq, ln_v_g[i].reshape(1, -1), ln_v_b[i].reshape(1, -1), sgu_w[i], sgu_b[i].T,
            pool_w[i].astype(BF16), pool_scale[i].reshape(1, -1))
        x2d = _merge(
            ya, yb, oc, od, proj, x2d, p[i].reshape(t, -1),
            w_a_out[i].astype(BF16), w_b_out[i].astype(BF16), w_c_out[i].astype(BF16),
            w_d_out[i].astype(BF16), w_o[i].astype(BF16), w_ple[i].astype(BF16),
            ple_norm_g[i].reshape(1, d), w_ple_gate[i].astype(BF16))
    return x2d.reshape(batch, seq, d)
```

```python
import functools
import math

import numpy as np
import jax
import jax.numpy as jnp
from jax import lax
from jax.experimental import pallas as pl
from jax.experimental.pallas import tpu as pltpu

F32 = jnp.float32
BF16 = jnp.bfloat16

EPS = 1e-6
NEG_INF = -1e30
CHUNK = 64
D_MODEL = 1024
PLE_DIM = 256
SGU_BLOCK = 128
GROUP_DIM = 128
N_GROUPS = 4
BRANCH_WIDTH = 512
POOL_WINDOWS = (2, 4, 8, 16)
POOL_HALO = 16
HEADS = 4
MLA_NOPE = 128
MLA_ROPE = 64
MLA_QK = MLA_NOPE + MLA_ROPE
MLA_Q_RANK = 256
MLA_KV_RANK = 128
MLA_QK_PAD = 256
HEAD_DIM = 128
ROPE_THETA = 10000.0

LANES = 128
VMEM_LIMIT = 56 * 1024 * 1024

OFF_GATES = 0
OFF_AU = 4096
OFF_AV = OFF_AU + 512
OFF_AZ = OFF_AV + 512
OFF_BX = OFF_AZ + 512
OFF_BZ = OFF_BX + 512
OFF_CZ = OFF_BZ + 512
OFF_DZ = OFF_CZ + 512
OFF_DQ = OFF_DZ + 512
OFF_DK = OFF_DQ + 512
OFF_DV = OFF_DK + 512
OFF_CQ = OFF_DV + 512
OFF_CKV = OFF_CQ + MLA_Q_RANK
OFF_CKR = OFF_CKV + MLA_KV_RANK
N_PROJ = OFF_CKR + 2 * MLA_ROPE

SB_UNDERFLOW = 105.0


def _proj_segments():
    sizes = (512, 512, 512, 512, 512, MLA_Q_RANK, MLA_KV_RANK, MLA_ROPE, 512, 512, 512, 512, 512, 4096)
    names = ("au", "av", "az", "bx", "bz", "cq", "ckv", "ckr", "cz", "dq", "dk", "dv", "dz", "gates")
    starts = dict(zip(names, np.cumsum((0,) + sizes[:-1])))
    size = dict(zip(names, sizes))
    order = ("gates", "au", "av", "az", "bx", "bz", "cz", "dz", "dq", "dk", "dv", "cq", "ckv", "ckr", "ckr")
    return [(int(starts[n]), int(size[n])) for n in order]


def _uq_segments():
    nope = [(h * MLA_QK, MLA_NOPE) for h in range(HEADS)]
    rope = [(h * MLA_QK + MLA_NOPE, MLA_ROPE) for h in range(HEADS)]
    return nope + rope


def _take_columns(w, segments):
    return jnp.concatenate([w[:, s:s + n] for s, n in segments], axis=1)


def _sigmoid(y):
    return 0.5 * jnp.tanh(0.5 * y) + 0.5


def _silu(z):
    return z * _sigmoid(z)


def _params(semantics):
    return pltpu.CompilerParams(dimension_semantics=semantics, vmem_limit_bytes=VMEM_LIMIT)


def _chunks(total, width):
    out, start = [], 0
    while start < total:
        size = min(width, total - start)
        out.append((start, size))
        start += size
    return out


def _inproj_kernel(x_ref, g_ref, w_ref, o_ref, h_sc):
    @pl.when(pl.program_id(1) == 0)
    def _():
        x = x_ref[...]
        ms = jnp.mean(x * x, axis=-1, keepdims=True)
        h_sc[...] = (x * lax.rsqrt(ms + EPS) * g_ref[...]).astype(BF16)

    h = h_sc[...]
    for start, size in _chunks(o_ref.shape[1], 512):
        cols = slice(start, start + size)
        o_ref[:, cols] = jnp.dot(h, w_ref[:, cols], preferred_element_type=F32).astype(BF16)


def _inproj(x2d, g, w):
    t, d = x2d.shape
    tm, tn = 1024, N_PROJ // 4
    return pl.pallas_call(
        _inproj_kernel,
        grid=(t // tm, N_PROJ // tn),
        in_specs=[
            pl.BlockSpec((tm, d), lambda i, j: (i, 0)),
            pl.BlockSpec((1, d), lambda i, j: (0, 0)),
            pl.BlockSpec((d, tn), lambda i, j: (0, j)),
        ],
        out_specs=pl.BlockSpec((tm, tn), lambda i, j: (i, j)),
        out_shape=jax.ShapeDtypeStruct((t, N_PROJ), BF16),
        scratch_shapes=[pltpu.VMEM((tm, d), BF16)],
        compiler_params=_params(("parallel", "arbitrary")),
        name="inproj",
    )(x2d, g, w)


def _rope_table_kernel(pos_ref, freq_ref, cos_ref, sin_ref):
    ang = freq_ref[...] * pos_ref[...].astype(F32)
    cos_ref[...] = jnp.cos(ang)
    sin_ref[...] = jnp.sin(ang)


def _rope_tables(positions):
    t = positions.size
    half = MLA_ROPE // 2
    tt = 2048
    freqs = (ROPE_THETA ** (-jnp.arange(half, dtype=F32) / half)).reshape(half, 1)
    cos_t, sin_t = pl.pallas_call(
        _rope_table_kernel,
        grid=(t // tt,),
        in_specs=[pl.BlockSpec((1, tt), lambda i: (0, i)),
                  pl.BlockSpec((half, 1), lambda i: (0, 0))],
        out_specs=[pl.BlockSpec((half, tt), lambda i: (0, i)),
                   pl.BlockSpec((half, tt), lambda i: (0, i))],
        out_shape=[jax.ShapeDtypeStruct((half, t), F32)] * 2,
        compiler_params=_params(("parallel",)),
        name="rope_tables",
    )(positions.reshape(1, t), freqs)
    sign = np.where((np.arange(LANES) % MLA_ROPE) < half, -1.0, 1.0).astype(np.float32)
    cos_l = jnp.tile(cos_t.T, (1, LANES // half))
    sin_l = jnp.tile(sin_t.T, (1, LANES // half)) * sign
    return cos_l, sin_l


def _mla_prep_kernel(cq_ref, ckv_ref, ckr_ref, cos_ref, sin_ref, cqg_ref, wuq_ref, ckvg_ref,
                     wukv_ref, qng_ref, kng_ref, q_ref, k_ref, v_ref):
    cos = cos_ref[...]
    sin = sin_ref[...]
    lane = lax.broadcasted_iota(jnp.int32, (1, LANES), 1)
    first_half = (lane % MLA_ROPE) < (MLA_ROPE // 2)
    low_group = lane < MLA_ROPE
    inv_qk = 1.0 / MLA_QK
    scale = MLA_QK ** -0.5 * math.log2(math.e)
    ones_column = jnp.where(lane == 0, 1.0, 0.0).astype(BF16)

    def rotary(xr):
        partner = jnp.where(first_half, pltpu.roll(xr, LANES - MLA_ROPE // 2, 1),
                            pltpu.roll(xr, MLA_ROPE // 2, 1))
        return xr * cos + partner * sin

    def rms(xf, g):
        return xf * lax.rsqrt(jnp.mean(xf * xf, axis=-1, keepdims=True) + EPS) * g

    qng = qng_ref[...]
    kng = kng_ref[...]
    q_raw = jnp.dot(rms(cq_ref[...].astype(F32), cqg_ref[...]).astype(BF16), wuq_ref[...],
                    preferred_element_type=F32)
    kv_raw = jnp.dot(rms(ckv_ref[...].astype(F32), ckvg_ref[...]).astype(BF16), wukv_ref[...],
                     preferred_element_type=F32)
    ckr = ckr_ref[...].astype(F32)
    ts = ckr.shape[0]

    ss_kr = 0.5 * jnp.sum(ckr * ckr, axis=-1, keepdims=True)
    kr_rot = rotary(ckr * kng[:, MLA_NOPE:])
    for h in range(HEADS):
        kn = kv_raw[:, 2 * h * HEAD_DIM:(2 * h + 1) * HEAD_DIM]
        r = lax.rsqrt((jnp.sum(kn * kn, axis=-1, keepdims=True) + ss_kr) * inv_qk + EPS)
        mine = low_group if h % 2 == 0 else jnp.logical_not(low_group)
        k_ref[h, :, 0:MLA_NOPE] = (kn * r * kng[:, :MLA_NOPE]).astype(BF16)
        k_ref[h, :, MLA_NOPE:] = jnp.where(mine, kr_rot * r, 0.0).astype(BF16)
        v_ref[h, :, 0:HEAD_DIM] = kv_raw[:, (2 * h + 1) * HEAD_DIM:(2 * h + 2) * HEAD_DIM].astype(BF16)
        v_ref[h, :, HEAD_DIM:] = jnp.broadcast_to(ones_column, (ts, LANES))

    for j in range(HEADS // 2):
        blk = q_raw[:, HEADS * MLA_NOPE + j * LANES:HEADS * MLA_NOPE + (j + 1) * LANES]
        sq = blk * blk
        rs = []
        for h in (2 * j, 2 * j + 1):
            mine = low_group if h % 2 == 0 else jnp.logical_not(low_group)
            qn = q_raw[:, h * MLA_NOPE:(h + 1) * MLA_NOPE]
            ss = jnp.sum(qn * qn, axis=-1, keepdims=True) + jnp.sum(
                jnp.where(mine, sq, 0.0), axis=-1, keepdims=True)
            r = lax.rsqrt(ss * inv_qk + EPS)
            rs.append(r)
            q_ref[h, :, 0:MLA_NOPE] = (qn * r * qng[:, :MLA_NOPE] * scale).astype(BF16)
        rot = rotary(blk * jnp.where(low_group, rs[0], rs[1]) * qng[:, MLA_NOPE:]) * scale
        q_ref[2 * j, :, MLA_NOPE:] = jnp.where(low_group, rot, 0.0).astype(BF16)
        q_ref[2 * j + 1, :, MLA_NOPE:] = jnp.where(low_group, 0.0, rot).astype(BF16)


def _mla_prep(proj, cos_l, sin_l, cqg, wuq, ckvg, wukv, qng, kng):
    t = proj.shape[0]
    ts = 512
    const = lambda shape: pl.BlockSpec(shape, lambda i: (0,) * len(shape))
    head_major = pl.BlockSpec((HEADS, ts, MLA_QK_PAD), lambda i: (0, i, 0))
    return pl.pallas_call(
        _mla_prep_kernel,
        grid=(t // ts,),
        in_specs=[
            pl.BlockSpec((ts, MLA_Q_RANK), lambda i: (i, OFF_CQ // MLA_Q_RANK)),
            pl.BlockSpec((ts, MLA_KV_RANK), lambda i: (i, OFF_CKV // MLA_KV_RANK)),
            pl.BlockSpec((ts, LANES), lambda i: (i, OFF_CKR // LANES)),
            pl.BlockSpec((ts, LANES), lambda i: (i, 0)),
            pl.BlockSpec((ts, LANES), lambda i: (i, 0)),
            const((1, MLA_Q_RANK)), const(wuq.shape), const((1, MLA_KV_RANK)), const(wukv.shape),
            const((1, MLA_QK_PAD)), const((1, MLA_QK_PAD)),
        ],
        out_specs=[head_major] * 3,
        out_shape=[jax.ShapeDtypeStruct((HEADS, t, MLA_QK_PAD), BF16)] * 3,
        compiler_params=_params(("parallel",)),
        name="mla_prep",
    )(proj, proj, proj, cos_l, sin_l, cqg, wuq, ckvg, wukv, qng, kng)


def _mla_attn_kernel(q_ref, k_ref, v_ref, o_ref, m_sc, acc_sc, sa_sc, sb_sc, *, tq):
    i = pl.program_id(2)
    tk = tq // 2
    everything = slice(0, tq)
    upper, lower = slice(0, tk), slice(tk, tq)

    def block_rows(b):
        return pl.ds(pl.multiple_of(b * tk, tk), tk)

    def scores(b, s_ref, rows=everything):
        s_ref[rows, :] = lax.dot_general(q_ref[rows, :], k_ref[block_rows(b), :],
                                         (((1,), (1,)), ((), ())), preferred_element_type=F32)

    def mask_diagonal(s_ref, rows):
        qc = lax.broadcasted_iota(jnp.int32, (tk, tk), 0) // CHUNK
        kc = lax.broadcasted_iota(jnp.int32, (tk, tk), 1) // CHUNK
        s_ref[rows, :] = jnp.where(kc <= qc, s_ref[rows, :], NEG_INF)

    def consume(b, s_ref, rows=everything):
        m_prev = m_sc[rows, :]
        m_new = jnp.maximum(m_prev, jnp.max(s_ref[rows, :], axis=-1, keepdims=True))
        p = jnp.exp2(s_ref[rows, :] - m_new).astype(BF16)
        acc_sc[rows, :] = (jnp.exp2(m_prev - m_new) * acc_sc[rows, :]
                           + jnp.dot(p, v_ref[block_rows(b), :], preferred_element_type=F32))
        m_sc[rows, :] = m_new

    m_sc[...] = jnp.full_like(m_sc, -jnp.inf)
    acc_sc[...] = jnp.zeros_like(acc_sc)
    scores(0, sa_sc)

    def pair(jj, carry):
        b = 2 * jj
        scores(b + 1, sb_sc)
        consume(b, sa_sc)
        scores(b + 2, sa_sc)
        consume(b + 1, sb_sc)
        return carry

    lax.fori_loop(0, i, pair, 0)
    scores(2 * i + 1, sb_sc, lower)
    mask_diagonal(sa_sc, upper)
    consume(2 * i, sa_sc)
    mask_diagonal(sb_sc, lower)
    consume(2 * i + 1, sb_sc, lower)

    acc = acc_sc[...]
    o_ref[...] = (acc[:, :HEAD_DIM] / acc[:, HEAD_DIM:HEAD_DIM + 1]).astype(BF16)


def _mla_attn(q, k, v, batch, seq):
    tq = 1024
    nq = seq // tq
    return pl.pallas_call(
        functools.partial(_mla_attn_kernel, tq=tq),
        grid=(batch, HEADS, nq),
        in_specs=[
            pl.BlockSpec((None, tq, MLA_QK_PAD), lambda b, h, i: (h, b * nq + i, 0)),
            pl.BlockSpec((None, seq, MLA_QK_PAD), lambda b, h, i: (h, b, 0)),
            pl.BlockSpec((None, seq, MLA_QK_PAD), lambda b, h, i: (h, b, 0)),
        ],
        out_specs=pl.BlockSpec((tq, HEAD_DIM), lambda b, h, i: (b * nq + i, h)),
        out_shape=jax.ShapeDtypeStruct((batch * seq, BRANCH_WIDTH), BF16),
        scratch_shapes=[pltpu.VMEM((tq, 1), F32), pltpu.VMEM((tq, MLA_QK_PAD), F32),
                        pltpu.VMEM((tq, tq // 2), F32), pltpu.VMEM((tq, tq // 2), F32)],
        compiler_params=_params(("parallel", "parallel", "arbitrary")),
        name="mla_attn",
    )(q, k, v)


def _sb_attn_kernel(q_ref, k_ref, v_ref, o_ref, acc_sc, run_sc, *, tq):
    i = pl.program_id(1)
    row =lax.broadcasted_iota(jnp.int32, (tq, tq), 0)
    col = lax.broadcasted_iota(jnp.int32, (tq, tq), 1)
    tri = jnp.where(row >= col, 1.0, 0.0).astype(BF16)
    strict = col < row

    def tile(j, masked):
        rows = pl.ds(pl.multiple_of(j * tq, tq), tq)
        heads = range(HEADS)
        cols = [slice(h * HEAD_DIM, (h + 1) * HEAD_DIM) for h in heads]
        z = [lax.dot_general(q_ref[:, cols[h]], k_ref[rows, cols[h]], (((1,), (1,)), ((), ())),
                             preferred_element_type=F32) for h in heads]
        hi, lo = [], []
        for h in heads:
            nz = -z[h]
            log_keep = jnp.minimum(nz, 0.0) - jnp.log(1.0 + jnp.exp(jnp.minimum(z[h], nz)))
            if masked:
                log_keep = jnp.where(strict, log_keep, 0.0)
            hi.append(log_keep.astype(BF16))
            lo.append((log_keep - hi[h].astype(F32)).astype(BF16))
        c = [jnp.dot(hi[h], tri, preferred_element_type=F32)
             + jnp.dot(lo[h], tri, preferred_element_type=F32) for h in heads]
        a = []
        for h in heads:
            ah = jnp.exp(z[h] + c[h] + run_sc[h])
            if masked:
                ah = jnp.where(strict, ah, 0.0)
            a.append(ah.astype(BF16))
        pv = [jnp.dot(a[h], v_ref[rows, cols[h]], preferred_element_type=F32) for h in heads]
        top = None
        for h in heads:
            acc_sc[:, cols[h]] += pv[h]
            run = run_sc[h] + c[h][:, 0:1]
            run_sc[h] = run
            top = run if top is None else jnp.maximum(top, run)
        return jnp.max(top)

    acc_sc[...] = jnp.zeros_like(acc_sc)
    run_sc[...] = jnp.zeros_like(run_sc)
    top = tile(i, True)

    def cond(state):
        j, top = state
        return jnp.logical_and(j >= 0, top > -SB_UNDERFLOW)

    def body(state):
        j, _ = state
        return j - 1, tile(j, False)

    lax.while_loop(cond, body, (i - 1, top))
    o_ref[...] = acc_sc[...].astype(BF16)


def _sb_attn(proj, batch, seq):
    tq = 256
    nq = seq // tq
    w = BRANCH_WIDTH
    resident = lambda off: pl.BlockSpec((seq, w), lambda b, i: (b, off // w),
                                        pipeline_mode=pl.Buffered(1))
    return pl.pallas_call(
        functools.partial(_sb_attn_kernel, tq=tq),
        grid=(batch, nq),
        in_specs=[
            pl.BlockSpec((tq, w), lambda b, i: (b * nq + i, OFF_DQ // w)),
            resident(OFF_DK),
            resident(OFF_DV),
        ],
        out_specs=pl.BlockSpec((tq, w), lambda b, i: (b * nq + i, 0)),
        out_shape=jax.ShapeDtypeStruct((batch * seq, w), BF16),
        scratch_shapes=[pltpu.VMEM((tq, w), F32), pltpu.VMEM((HEADS, tq, 1), F32)],
        compiler_params=_params(("parallel", "arbitrary")),
        name="sb_attn",
    )(proj, proj, proj)


def _local_kernel(au_ref, av_ref, az_ref, bx_ref, halo_ref, bz_ref, lng_ref, lnb_ref, sw_ref, sb_ref,
                  pw_ref, ps_ref, ya_ref, yb_ref, *, tm, seq):
    i = pl.program_id(0)

    v = av_ref[...].astype(F32)
    mu = jnp.mean(v, axis=-1, keepdims=True)
    vc = v - mu
    vn = (vc * lax.rsqrt(jnp.mean(vc * vc, axis=-1, keepdims=True) + EPS) * lng_ref[...]
          + lnb_ref[...]).astype(BF16)
    tpos = lax.broadcasted_iota(jnp.int32, (SGU_BLOCK, SGU_BLOCK), 0) // CHUNK
    spos = lax.broadcasted_iota(jnp.int32, (SGU_BLOCK, SGU_BLOCK), 1) // CHUNK
    bias_t = sb_ref[...]
    for g in range(N_GROUPS):
        w = jnp.where(spos <= tpos, sw_ref[g], 0.0).astype(BF16)
        cols = slice(g * GROUP_DIM, (g + 1) * GROUP_DIM)
        for n in range(tm // SGU_BLOCK):
            rows = slice(n * SGU_BLOCK, (n + 1) * SGU_BLOCK)
            mixed = jnp.dot(w, vn[rows, cols], preferred_element_type=F32) + bias_t[:, g:g + 1]
            ya_ref[rows, cols] = (au_ref[rows, cols].astype(F32) * mixed
                                  * _silu(az_ref[rows, cols].astype(F32))).astype(BF16)

    start = (i * tm) % seq
    halo = jnp.where(start == 0, 0.0, halo_ref[...].astype(F32))
    xfull = jnp.concatenate([halo, bx_ref[...].astype(F32)], axis=0)
    pos = start + lax.broadcasted_iota(jnp.int32, (tm, 1), 0)
    for g, win in enumerate(POOL_WINDOWS):
        cols = slice(g * GROUP_DIM, (g + 1) * GROUP_DIM)
        xg = xfull[:, cols]
        s, shift = xg, 1
        while shift < win:
            s = s + pltpu.roll(s, shift, 0)
            shift *= 2
        count = jnp.minimum(pos + 1, win).astype(F32)
        pooled = s[POOL_HALO:] / count - xg[POOL_HALO:]
        y = jnp.dot(pooled.astype(BF16), pw_ref[g], preferred_element_type=F32) * ps_ref[:, cols]
        yb_ref[:, cols] = (y * _silu(bz_ref[:, cols].astype(F32))).astype(BF16)


def _local_mixers(proj, seq, lng, lnb, sgu_w, sgu_bt, pool_w, pool_scale):
    t = proj.shape[0]
    tm = 512
    w = BRANCH_WIDTH
    const = lambda shape: pl.BlockSpec(shape, lambda i: (0,) * len(shape))
    col = lambda off: pl.BlockSpec((tm, w), lambda i: (i, off // w))
    halo = pl.BlockSpec((POOL_HALO, w),
                        lambda i: (jnp.maximum(i * (tm // POOL_HALO) - 1, 0), OFF_BX // w))
    return pl.pallas_call(
        functools.partial(_local_kernel, tm=tm, seq=seq),
        grid=(t // tm,),
        in_specs=[col(OFF_AU), col(OFF_AV), col(OFF_AZ), col(OFF_BX), halo, col(OFF_BZ),
                  const((1, w)), const((1, w)), const(sgu_w.shape), const(sgu_bt.shape),
                  const(pool_w.shape), const((1, w))],
        out_specs=[pl.BlockSpec((tm, w), lambda i: (i, 0))] * 2,
        out_shape=[jax.ShapeDtypeStruct((t, w), BF16)] * 2,
        compiler_params=_params(("parallel",)),
        name="local_mixers",
    )(proj, proj, proj, proj, proj, proj, lng, lnb, sgu_w, sgu_bt, pool_w, pool_scale)


def _merge_kernel(ya_ref, yb_ref, oc_ref, od_ref, cz_ref, dz_ref, gates_ref, x_ref, p_ref,
                  wa_ref, wb_ref, wc_ref, wd_ref, wo_ref, wple_ref, pleg_ref, wpg_ref, o_ref):
    yc = (oc_ref[...].astype(F32) * _silu(cz_ref[...].astype(F32))).astype(BF16)
    yd = (od_ref[...].astype(F32) * _silu(dz_ref[...].astype(F32))).astype(BF16)
    merged = None
    for n, (y, w_ref) in enumerate(((ya_ref[...], wa_ref), (yb_ref[...], wb_ref),
                                    (yc, wc_ref), (yd, wd_ref))):
        gate = _sigmoid(gates_ref[:, n * D_MODEL:(n + 1) * D_MODEL].astype(F32))
        term = gate * jnp.dot(y, w_ref[...], preferred_element_type=F32)
        merged = term if merged is None else merged + term
    x1 = x_ref[...] + jnp.dot(merged.astype(BF16), wo_ref[...], preferred_element_type=F32)
    e = jnp.dot(p_ref[...].astype(BF16), wple_ref[...], preferred_element_type=F32)
    n1 = (x1 * lax.rsqrt(jnp.mean(x1 * x1, axis=-1, keepdims=True) + EPS) * pleg_ref[...]).astype(BF16)
    o_ref[...] = x1 + _sigmoid(jnp.dot(n1, wpg_ref[...], preferred_element_type=F32)) * e


def _merge(ya, yb, oc, od, proj, x2d, p2d, layer, wa, wb, wc, wd, wo, wple, pleg, wpg):
    t, d = x2d.shape
    tm = 512
    w = BRANCH_WIDTH
    const = lambda shape: pl.BlockSpec(shape, lambda i: (0,) * len(shape))
    row = lambda width: pl.BlockSpec((tm, width), lambda i: (i, 0))
    return pl.pallas_call(
        _merge_kernel,
        grid=(t // tm,),
        in_specs=[row(w), row(w), row(w), row(w),
                  pl.BlockSpec((tm, w), lambda i: (i, OFF_CZ // w)),
                  pl.BlockSpec((tm, w), lambda i: (i, OFF_DZ // w)),
                  pl.BlockSpec((tm, 4 * d), lambda i: (i, OFF_GATES // (4 * d))),
                  row(d),
                  pl.BlockSpec((tm, PLE_DIM), lambda i: (layer * (t // tm) + i, 0)),
                  const(wa.shape), const(wb.shape), const(wc.shape), const(wd.shape),
                  const(wo.shape), const(wple.shape), const((1, d)), const(wpg.shape)],
        out_specs=row(d),
        out_shape=jax.ShapeDtypeStruct((t, d), F32),
        compiler_params=_params(("parallel",)),
        name="merge_out",
    )(ya, yb, oc, od, proj, proj, proj, x2d, p2d, wa, wb, wc, wd, wo, wple, pleg, wpg)


def kernel(x, p, positions, norm_g, w_in, ln_v_g, ln_v_b, sgu_w, sgu_b, w_a_out, pool_w, pool_scale,
           w_b_out, cq_norm_g, w_uq, ckv_norm_g, w_ukv, q_norm_g, k_norm_g, w_c_out, w_d_out, w_o,
           w_ple, ple_norm_g, w_ple_gate):
    batch, seq, d = x.shape
    depth = w_in.shape[0]
    t = batch * seq
    x2d = x.reshape(t, d)
    p2d = p.reshape(depth * t, PLE_DIM)
    assert d == D_MODEL and seq % 1024 == 0 and t % 2048 == 0, (batch, seq, d)
    proj_segments = _proj_segments()
    uq_segments = _uq_segments()
    col_scale = np.ones((1, N_PROJ), np.float32)
    col_scale[:, OFF_DQ:OFF_DQ + BRANCH_WIDTH] = HEAD_DIM ** -0.5
    cos_l, sin_l = _rope_tables(positions)

    def qk_gain(g):
        rope = g[MLA_NOPE:]
        return jnp.concatenate([g[:MLA_NOPE], rope, rope]).reshape(1, MLA_QK_PAD)

    for i in range(depth):
        w_in_i = (_take_columns(w_in[i], proj_segments) * col_scale).astype(BF16)
        proj = _inproj(x2d, norm_g[i].reshape(1, d), w_in_i)
        q, k, v = _mla_prep(
            proj, cos_l, sin_l, cq_norm_g[i].reshape(1, -1),
            _take_columns(w_uq[i], uq_segments).astype(BF16),
            ckv_norm_g[i].reshape(1, -1), w_ukv[i].astype(BF16), qk_gain(q_norm_g[i]), qk_gain(k_norm_g[i]))
        oc = _mla_attn(q, k, v, batch, seq)
        od = _sb_attn(proj, batch, seq)
        ya, yb = _local_mixers(
            proj, seq, ln_v_g[i].reshape(1, -1), ln_v_b[i].reshape(1, -1), sgu_w[i], sgu_b[i].T,
            pool_w[i].astype(BF16), pool_scale[i].reshape(1, -1))
        x2d = _merge(
            ya, yb, oc, od, proj, x2d, p2d, i,
            w_a_out[i].astype(BF16), w_b_out[i].astype(BF16), w_c_out[i].astype(BF16),
            w_d_out[i].astype(BF16), w_o[i].astype(BF16), w_ple[i].astype(BF16),
            ple_norm_g[i].reshape(1, d), w_ple_gate[i].astype(BF16))
    return x2d.reshape(batch, seq, d)
```

```python
import functools
import math

import numpy as np
import jax
import jax.numpy as jnp
from jax import lax
from jax.experimental import pallas as pl
from jax.experimental.pallas import tpu as pltpu

F32 = jnp.float32
BF16 = jnp.bfloat16

EPS = 1e-6
NEG_INF = -1e30
CHUNK = 64
D_MODEL = 1024
PLE_DIM = 256
SGU_BLOCK = 128
GROUP_DIM = 128
N_GROUPS = 4
BRANCH_WIDTH = 512
POOL_WINDOWS = (2, 4, 8, 16)
POOL_HALO = 16
HEADS = 4
MLA_NOPE = 128
MLA_ROPE = 64
MLA_QK = MLA_NOPE + MLA_ROPE
MLA_Q_RANK = 256
MLA_KV_RANK = 128
MLA_QK_PAD = 256
HEAD_DIM = 128
ROPE_THETA = 10000.0

LANES = 128
VMEM_LIMIT = 56 * 1024 * 1024

OFF_GATES = 0
OFF_AU = 4096
OFF_AV = OFF_AU + 512
OFF_AZ = OFF_AV + 512
OFF_BX = OFF_AZ + 512
OFF_BZ = OFF_BX + 512
OFF_CZ = OFF_BZ + 512
OFF_DZ = OFF_CZ + 512
OFF_DQ = OFF_DZ + 512
OFF_DK = OFF_DQ + 512
OFF_DV = OFF_DK + 512
OFF_CQ = OFF_DV + 512
OFF_CKV = OFF_CQ + MLA_Q_RANK
OFF_CKR = OFF_CKV + MLA_KV_RANK
N_PROJ = OFF_CKR + 2 * MLA_ROPE

SB_UNDERFLOW_LOG2 = 152.0


def _reference_layout():
    sizes = (512, 512, 512, 512, 512, MLA_Q_RANK, MLA_KV_RANK, MLA_ROPE, 512, 512, 512, 512, 512, 4096)
    names = ("au", "av", "az", "bx", "bz", "cq", "ckv", "ckr", "cz", "dq", "dk", "dv", "dz", "gates")
    starts = np.cumsum((0,) + sizes[:-1])
    return {n: (int(s), int(z)) for n, s, z in zip(names, starts, sizes)}


def _proj_segments():
    layout = _reference_layout()
    order = ("gates", "au", "av", "az", "bx", "bz", "cz", "dz", "dq", "dk", "dv", "cq", "ckv", "ckr", "ckr")
    return [layout[n] for n in order]


def _uq_segments():
    nope = [(h * MLA_QK, MLA_NOPE) for h in range(HEADS)]
    rope = [(h * MLA_QK + MLA_NOPE, MLA_ROPE) for h in range(HEADS)]
    return nope + rope


def _take_columns(w, segments):
    return jnp.concatenate([w[..., s:s + n] for s, n in segments], axis=-1)


def _sigmoid(y):
    return 0.5 * jnp.tanh(0.5 * y) + 0.5


def _silu(z):
    return z * _sigmoid(z)


def _params(semantics):
    return pltpu.CompilerParams(dimension_semantics=semantics, vmem_limit_bytes=VMEM_LIMIT)


def _chunks(total, width):
    out, start = [], 0
    while start < total:
        size = min(width, total - start)
        out.append((start, size))
        start += size
    return out


def _inproj_kernel(x_ref, g_ref, w_ref, o_ref, h_sc):
    @pl.when(pl.program_id(1) == 0)
    def _():
        x = x_ref[...]
        ms = jnp.mean(x * x, axis=-1, keepdims=True)
        h_sc[...] = (x * lax.rsqrt(ms + EPS) * g_ref[...]).astype(BF16)

    h = h_sc[...]
    for start, size in _chunks(o_ref.shape[1], 512):
        cols = slice(start, start + size)
        o_ref[:, cols] = jnp.dot(h, w_ref[:, cols], preferred_element_type=F32).astype(BF16)


def _layer_spec(stacked, layer):
    shape = stacked.shape[1:]
    return pl.BlockSpec((None,) + shape, lambda *_: (layer,) + (0,) * len(shape))


def _inproj(x2d, g, w, layer):
    t, d = x2d.shape
    tm, tn = 1024, N_PROJ // 2
    return pl.pallas_call(
        _inproj_kernel,
        grid=(t // tm, N_PROJ // tn),
        in_specs=[
            pl.BlockSpec((tm, d), lambda i, j: (i, 0)),
            _layer_spec(g, layer),
            pl.BlockSpec((None, d, tn), lambda i, j: (layer, 0, j)),
        ],
        out_specs=pl.BlockSpec((tm, tn), lambda i, j: (i, j)),
        out_shape=jax.ShapeDtypeStruct((t, N_PROJ), BF16),
        scratch_shapes=[pltpu.VMEM((tm, d), BF16)],
        compiler_params=_params(("parallel", "arbitrary")),
        name="inproj",
    )(x2d, g, w)


def _rope_table_kernel(pos_ref, freq_ref, sign_ref, cos_ref, sin_ref):
    ang = freq_ref[...] * pos_ref[...].astype(F32)
    reps = LANES // ang.shape[0]
    cos_ref[...] = jnp.concatenate([jnp.cos(ang)] * reps, axis=0).T
    sin_ref[...] = (jnp.concatenate([jnp.sin(ang)] * reps, axis=0) * sign_ref[...]).T


def _rope_tables(positions):
    t = positions.size
    half = MLA_ROPE // 2
    tt = 2048
    freqs = (ROPE_THETA ** (-jnp.arange(half, dtype=F32) / half)).reshape(half, 1)
    sign = np.where((np.arange(LANES) % MLA_ROPE) < half, -1.0, 1.0).astype(np.float32).reshape(LANES, 1)
    return pl.pallas_call(
        _rope_table_kernel,
        grid=(t // tt,),
        in_specs=[pl.BlockSpec((1, tt), lambda i: (0, i)),
                  pl.BlockSpec((half, 1), lambda i: (0, 0)),
                  pl.BlockSpec((LANES, 1), lambda i: (0, 0))],
        out_specs=[pl.BlockSpec((tt, LANES), lambda i: (i, 0)),
                   pl.BlockSpec((tt, LANES), lambda i: (i, 0))],
        out_shape=[jax.ShapeDtypeStruct((t, LANES), F32)] * 2,
        compiler_params=_params(("parallel",)),
        name="rope_tables",
    )(positions.reshape(1, t), freqs, sign)


def _mla_prep_kernel(cq_ref, ckv_ref, ckr_ref, cos_ref, sin_ref, cqg_ref, wuq_ref, ckvg_ref,
                     wukv_ref, qng_ref, kng_ref, q_ref, k_ref, v_ref):
    cos = cos_ref[...]
    sin = sin_ref[...]
    lane = lax.broadcasted_iota(jnp.int32, (1, LANES), 1)
    first_half = (lane % MLA_ROPE) < (MLA_ROPE // 2)
    low_group = lane < MLA_ROPE
    inv_qk = 1.0 / MLA_QK
    scale = MLA_QK ** -0.5 * math.log2(math.e)
    ones_column = jnp.where(lane == 0, 1.0, 0.0).astype(BF16)

    def rotary(xr):
        partner = jnp.where(first_half, pltpu.roll(xr, LANES - MLA_ROPE // 2, 1),
                            pltpu.roll(xr, MLA_ROPE // 2, 1))
        return xr * cos + partner * sin

    def rms(xf, g):
        return xf * lax.rsqrt(jnp.mean(xf * xf, axis=-1, keepdims=True) + EPS) * g

    qng = qng_ref[...]
    kng = kng_ref[...]
    q_raw = jnp.dot(rms(cq_ref[...].astype(F32), cqg_ref[...]).astype(BF16), wuq_ref[...],
                    preferred_element_type=F32)
    kv_raw = jnp.dot(rms(ckv_ref[...].astype(F32), ckvg_ref[...]).astype(BF16), wukv_ref[...],
                     preferred_element_type=F32)
    ckr = ckr_ref[...].astype(F32)
    ts = ckr.shape[0]

    ss_kr = 0.5 * jnp.sum(ckr * ckr, axis=-1, keepdims=True)
    kr_rot = rotary(ckr * kng[:, MLA_NOPE:])
    for h in range(HEADS):
        kn = kv_raw[:, 2 * h * HEAD_DIM:(2 * h + 1) * HEAD_DIM]
        r = lax.rsqrt((jnp.sum(kn * kn, axis=-1, keepdims=True) + ss_kr) * inv_qk + EPS)
        mine = low_group if h % 2 == 0 else jnp.logical_not(low_group)
        k_ref[h, :, 0:MLA_NOPE] = (kn * r * kng[:, :MLA_NOPE]).astype(BF16)
        k_ref[h, :, MLA_NOPE:] = jnp.where(mine, kr_rot * r, 0.0).astype(BF16)
        v_ref[h, :, 0:HEAD_DIM] = kv_raw[:, (2 * h + 1) * HEAD_DIM:(2 * h + 2) * HEAD_DIM].astype(BF16)
        v_ref[h, :, HEAD_DIM:] = jnp.broadcast_to(ones_column, (ts, LANES))

    for j in range(HEADS // 2):
        blk = q_raw[:, HEADS * MLA_NOPE + j * LANES:HEADS * MLA_NOPE + (j + 1) * LANES]
        sq = blk * blk
        rs = []
        for h in (2 * j, 2 * j + 1):
            mine = low_group if h % 2 == 0 else jnp.logical_not(low_group)
            qn = q_raw[:, h * MLA_NOPE:(h + 1) * MLA_NOPE]
            ss = jnp.sum(qn * qn, axis=-1, keepdims=True) + jnp.sum(
                jnp.where(mine, sq, 0.0), axis=-1, keepdims=True)
            r = lax.rsqrt(ss * inv_qk + EPS)
            rs.append(r)
            q_ref[h, :, 0:MLA_NOPE] = (qn * r * qng[:, :MLA_NOPE] * scale).astype(BF16)
        rot = rotary(blk * jnp.where(low_group, rs[0], rs[1]) * qng[:, MLA_NOPE:]) * scale
        q_ref[2 * j, :, MLA_NOPE:] = jnp.where(low_group, rot, 0.0).astype(BF16)
        q_ref[2 * j + 1, :, MLA_NOPE:] = jnp.where(low_group, 0.0, rot).astype(BF16)


def _mla_prep(proj, cos_l, sin_l, layer, cqg, wuq, ckvg, wukv, qng, kng):
    t = proj.shape[0]
    ts = 512
    head_major = pl.BlockSpec((HEADS, ts, MLA_QK_PAD), lambda i: (0, i, 0))
    return pl.pallas_call(
        _mla_prep_kernel,
        grid=(t // ts,),
        in_specs=[
            pl.BlockSpec((ts, MLA_Q_RANK), lambda i: (i, OFF_CQ // MLA_Q_RANK)),
            pl.BlockSpec((ts, MLA_KV_RANK), lambda i: (i, OFF_CKV // MLA_KV_RANK)),
            pl.BlockSpec((ts, LANES), lambda i: (i, OFF_CKR // LANES)),
            pl.BlockSpec((ts, LANES), lambda i: (i, 0)),
            pl.BlockSpec((ts, LANES), lambda i: (i, 0)),
        ] + [_layer_spec(a, layer) for a in (cqg, wuq, ckvg, wukv, qng, kng)],
        out_specs=[head_major] * 3,
        out_shape=[jax.ShapeDtypeStruct((HEADS, t, MLA_QK_PAD), BF16)] * 3,
        compiler_params=_params(("parallel",)),
        name="mla_prep",
    )(proj, proj, proj, cos_l, sin_l, cqg, wuq, ckvg, wukv, qng, kng)


def _mla_attn_kernel(q_ref, k_ref, v_ref, o_ref, m_sc, acc_sc, sa_sc, sb_sc, *, tq):
    i = pl.program_id(2)
    tk = tq // 2
    everything = slice(0, tq)
    upper, lower = slice(0, tk), slice(tk, tq)

    def block_rows(b):
        return pl.ds(pl.multiple_of(b * tk, tk), tk)

    def scores(b, s_ref, rows=everything):
        s_ref[rows, :] = lax.dot_general(q_ref[rows, :], k_ref[block_rows(b), :],
                                         (((1,), (1,)), ((), ())), preferred_element_type=F32)

    def mask_diagonal(s_ref, rows):
        qc = lax.broadcasted_iota(jnp.int32, (tk, tk), 0) // CHUNK
        kc = lax.broadcasted_iota(jnp.int32, (tk, tk), 1) // CHUNK
        s_ref[rows, :] = jnp.where(kc <= qc, s_ref[rows, :], NEG_INF)

    def consume(b, s_ref, rows=everything):
        m_prev = m_sc[rows, :]
        m_new = jnp.maximum(m_prev, jnp.max(s_ref[rows, :], axis=-1, keepdims=True))
        p = jnp.exp2(s_ref[rows, :] - pltpu.repeat(m_new, tk // LANES, axis=1)).astype(BF16)
        alpha = pltpu.repeat(jnp.exp2(m_prev - m_new), MLA_QK_PAD // LANES, axis=1)
        acc_sc[rows, :] = (alpha * acc_sc[rows, :]
                           + jnp.dot(p, v_ref[block_rows(b), :], preferred_element_type=F32))
        m_sc[rows, :] = m_new

    m_sc[...] = jnp.full_like(m_sc, -jnp.inf)
    acc_sc[...] = jnp.zeros_like(acc_sc)
    scores(0, sa_sc, upper)
    scores(0, sa_sc, lower)

    def pairs(first_block, n_pairs):
        for n in range(n_pairs):
            b = first_block + 2 * n
            scores(b + 1, sb_sc)
            consume(b, sa_sc)
            scores(b + 2, sa_sc)
            consume(b + 1, sb_sc)

    def quad(jj, carry):
        pairs(4 * jj, 2)
        return carry

    lax.fori_loop(0, i // 2, quad, 0)

    @pl.when(i % 2 == 1)
    def _():
        pairs(2 * (i - 1), 1)

    scores(2 * i + 1, sb_sc, lower)
    mask_diagonal(sa_sc, upper)
    consume(2 * i, sa_sc)
    mask_diagonal(sb_sc, lower)
    consume(2 * i + 1, sb_sc, lower)

    acc = acc_sc[...]
    o_ref[...] = (acc[:, :HEAD_DIM] / acc[:, HEAD_DIM:HEAD_DIM + 1]).astype(BF16)


def _mla_attn(q, k, v, batch, seq):
    tq = 1024
    nq = seq // tq
    return pl.pallas_call(
        functools.partial(_mla_attn_kernel, tq=tq),
        grid=(batch, HEADS, nq),
        in_specs=[
            pl.BlockSpec((None, tq, MLA_QK_PAD), lambda b, h, i: (h, b * nq + i, 0)),
            pl.BlockSpec((None, seq, MLA_QK_PAD), lambda b, h, i: (h, b, 0)),
            pl.BlockSpec((None, seq, MLA_QK_PAD), lambda b, h, i: (h, b, 0)),
        ],
        out_specs=pl.BlockSpec((tq, HEAD_DIM), lambda b, h, i: (b * nq + i, h)),
        out_shape=jax.ShapeDtypeStruct((batch * seq, BRANCH_WIDTH), BF16),
        scratch_shapes=[pltpu.VMEM((tq, LANES), F32), pltpu.VMEM((tq, MLA_QK_PAD), F32),
                        pltpu.VMEM((tq, tq // 2), F32), pltpu.VMEM((tq, tq // 2), F32)],
        compiler_params=_params(("parallel", "parallel", "arbitrary")),
        name="mla_attn",
    )(q, k, v)


def _sb_attn_kernel(q_ref, k_ref, v_ref, o_ref, acc_sc, run_sc, *, tq):
    i = pl.program_id(1)
    row = lax.broadcasted_iota(jnp.int32, (2 * tq, tq), 0)
    col = lax.broadcasted_iota(jnp.int32, (2 * tq, tq), 1)
    tri2 = jnp.where((row % tq) >= col, 1.0, 0.0).astype(BF16)
    strict = (lax.broadcasted_iota(jnp.int32, (tq, tq), 1)
              < lax.broadcasted_iota(jnp.int32, (tq, tq), 0))
    sign_bit = jnp.uint32(0x80000000)

    def tile(j, masked):
        rows = pl.ds(pl.multiple_of(j * tq, tq), tq)
        heads = range(HEADS)
        cols = [slice(h * HEAD_DIM, (h + 1) * HEAD_DIM) for h in heads]
        z = [lax.dot_general(q_ref[:, cols[h]], k_ref[rows, cols[h]], (((1,), (1,)), ((), ())),
                             preferred_element_type=F32) for h in heads]
        split, total = [], []
        for h in heads:
            neg_abs = lax.bitcast_convert_type(lax.bitcast_convert_type(z[h], jnp.uint32) | sign_bit, F32)
            pen = jnp.maximum(z[h], 0.0) + jnp.log2(1.0 + jnp.exp2(neg_abs))
            if masked:
                pen = jnp.where(strict, pen, 0.0)
            hi = pen.astype(BF16)
            lo = (pen - hi.astype(F32)).astype(BF16)
            split.append(jnp.concatenate([hi, lo], axis=1))
            total.append(jnp.sum(pen, axis=-1, keepdims=True))
        c = [jnp.dot(split[h], tri2, preferred_element_type=F32) for h in heads]
        a = []
        for h in heads:
            ah = jnp.exp2(z[h] - c[h] - pltpu.repeat(run_sc[h], tq // LANES, axis=1))
            if masked:
                ah = jnp.where(strict, ah, 0.0)
            a.append(ah.astype(BF16))
        pv = [jnp.dot(a[h], v_ref[rows, cols[h]], preferred_element_type=F32) for h in heads]
        low = None
        for h in heads:
            acc_sc[:, cols[h]] += pv[h]
            run = run_sc[h] + total[h]
            run_sc[h] = run
            low = run if low is None else jnp.minimum(low, run)
        return jnp.min(low)

    acc_sc[...] = jnp.zeros_like(acc_sc)
    run_sc[...] = jnp.zeros_like(run_sc)
    low = tile(i, True)

    def cond(state):
        j, low = state
        return jnp.logical_and(j >= 0, low < SB_UNDERFLOW_LOG2)

    def body(state):
        j, _ = state
        return j - 1, tile(j, False)

    lax.while_loop(cond, body, (i - 1, low))
    o_ref[...] = acc_sc[...].astype(BF16)


def _sb_attn(proj, batch, seq):
    tq = 256
    nq = seq // tq
    w = BRANCH_WIDTH
    resident = lambda off: pl.BlockSpec((seq, w), lambda b, i: (b, off // w),
                                        pipeline_mode=pl.Buffered(1))
    return pl.pallas_call(
        functools.partial(_sb_attn_kernel, tq=tq),
        grid=(batch, nq),
        in_specs=[
            pl.BlockSpec((tq, w), lambda b, i: (b * nq + i, OFF_DQ // w)),
            resident(OFF_DK),
            resident(OFF_DV),
        ],
        out_specs=pl.BlockSpec((tq, w), lambda b, i: (b * nq + i, 0)),
        out_shape=jax.ShapeDtypeStruct((batch * seq, w), BF16),
        scratch_shapes=[pltpu.VMEM((tq, w), F32), pltpu.VMEM((HEADS, tq, LANES), F32)],
        compiler_params=_params(("parallel", "arbitrary")),
        name="sb_attn",
    )(proj, proj, proj)


def _local_kernel(au_ref, av_ref, az_ref, bx_ref, halo_ref, bz_ref, lng_ref, lnb_ref, sw_ref, sb_ref,
                  pw_ref, ps_ref, ya_ref, yb_ref, *, tm, seq):
    i = pl.program_id(0)

    v = av_ref[...].astype(F32)
    mu = jnp.mean(v, axis=-1, keepdims=True)
    vc = v - mu
    vn = (vc * lax.rsqrt(jnp.mean(vc * vc, axis=-1, keepdims=True) + EPS) * lng_ref[...]
          + lnb_ref[...]).astype(BF16)
    tpos = lax.broadcasted_iota(jnp.int32, (SGU_BLOCK, SGU_BLOCK), 0) // CHUNK
    spos = lax.broadcasted_iota(jnp.int32, (SGU_BLOCK, SGU_BLOCK), 1) // CHUNK
    bias_t = sb_ref[...]
    for g in range(N_GROUPS):
        w = jnp.where(spos <= tpos, sw_ref[g], 0.0).astype(BF16)
        cols = slice(g * GROUP_DIM, (g + 1) * GROUP_DIM)
        for n in range(tm // SGU_BLOCK):
            rows = slice(n * SGU_BLOCK, (n + 1) * SGU_BLOCK)
            mixed = jnp.dot(w, vn[rows, cols], preferred_element_type=F32) + bias_t[:, g:g + 1]
            ya_ref[rows, cols] = (au_ref[rows, cols].astype(F32) * mixed
                                  * _silu(az_ref[rows, cols].astype(F32))).astype(BF16)

    start = (i * tm) % seq
    halo = jnp.where(start == 0, 0.0, halo_ref[...].astype(F32))
    xfull = jnp.concatenate([halo, bx_ref[...].astype(F32)], axis=0)
    pos = start + lax.broadcasted_iota(jnp.int32, (tm, 1), 0)
    for g, win in enumerate(POOL_WINDOWS):
        cols = slice(g * GROUP_DIM, (g + 1) * GROUP_DIM)
        xg = xfull[:, cols]
        s, shift = xg, 1
        while shift < win:
            s = s + pltpu.roll(s, shift, 0)
            shift *= 2
        count = jnp.minimum(pos + 1, win).astype(F32)
        pooled = s[POOL_HALO:] / count - xg[POOL_HALO:]
        y = jnp.dot(pooled.astype(BF16), pw_ref[g], preferred_element_type=F32) * ps_ref[:, cols]
        yb_ref[:, cols] = (y * _silu(bz_ref[:, cols].astype(F32))).astype(BF16)


def _local_mixers(proj, seq, layer, lng, lnb, sgu_w, sgu_bt, pool_w, pool_scale):
    t = proj.shape[0]
    tm = 512
    w = BRANCH_WIDTH
    col = lambda off: pl.BlockSpec((tm, w), lambda i: (i, off // w))
    halo = pl.BlockSpec((POOL_HALO, w),
                        lambda i: (jnp.maximum(i * (tm // POOL_HALO) - 1, 0), OFF_BX // w))
    return pl.pallas_call(
        functools.partial(_local_kernel, tm=tm, seq=seq),
        grid=(t // tm,),
        in_specs=[col(OFF_AU), col(OFF_AV), col(OFF_AZ), col(OFF_BX), halo, col(OFF_BZ)]
        + [_layer_spec(a, layer) for a in (lng, lnb, sgu_w, sgu_bt, pool_w, pool_scale)],
        out_specs=[pl.BlockSpec((tm, w), lambda i: (i, 0))] * 2,
        out_shape=[jax.ShapeDtypeStruct((t, w), BF16)] * 2,
        compiler_params=_params(("parallel",)),
        name="local_mixers",
    )(proj, proj, proj, proj, proj, proj, lng, lnb, sgu_w, sgu_bt, pool_w, pool_scale)


def _merge_kernel(ya_ref, yb_ref, oc_ref, od_ref, cz_ref, dz_ref, gates_ref, x_ref, p_ref,
                  wa_ref, wb_ref, wc_ref, wd_ref, wo_ref, wple_ref, pleg_ref, wpg_ref, o_ref):
    yc = (oc_ref[...].astype(F32) * _silu(cz_ref[...].astype(F32))).astype(BF16)
    yd = (od_ref[...].astype(F32) * _silu(dz_ref[...].astype(F32))).astype(BF16)
    merged = None
    for n, (y, w_ref) in enumerate(((ya_ref[...], wa_ref), (yb_ref[...], wb_ref),
                                    (yc, wc_ref), (yd, wd_ref))):
        gate = _sigmoid(gates_ref[:, n * D_MODEL:(n + 1) * D_MODEL].astype(F32))
        term = gate * jnp.dot(y, w_ref[...], preferred_element_type=F32)
        merged = term if merged is None else merged + term
    x1 = x_ref[...] + jnp.dot(merged.astype(BF16), wo_ref[...], preferred_element_type=F32)
    e = jnp.dot(p_ref[...].astype(BF16), wple_ref[...], preferred_element_type=F32)
    n1 = (x1 * lax.rsqrt(jnp.mean(x1 * x1, axis=-1, keepdims=True) + EPS) * pleg_ref[...]).astype(BF16)
    o_ref[...] = x1 + _sigmoid(jnp.dot(n1, wpg_ref[...], preferred_element_type=F32)) * e


def _merge(ya, yb, oc, od, proj, x2d, p2d, layer, wa, wb, wc, wd, wo, wple, pleg, wpg):
    t, d = x2d.shape
    tm = 512
    w = BRANCH_WIDTH
    row = lambda width: pl.BlockSpec((tm, width), lambda i: (i, 0))
    return pl.pallas_call(
        _merge_kernel,
        grid=(t // tm,),
        in_specs=[row(w), row(w), row(w), row(w),
                  pl.BlockSpec((tm, w), lambda i: (i, OFF_CZ // w)),
                  pl.BlockSpec((tm, w), lambda i: (i, OFF_DZ // w)),
                  pl.BlockSpec((tm, 4 * d), lambda i: (i, OFF_GATES // (4 * d))),
                  row(d),
                  pl.BlockSpec((tm, PLE_DIM), lambda i: (layer * (t // tm) + i, 0))]
        + [_layer_spec(a, layer) for a in (wa, wb, wc, wd, wo, wple, pleg, wpg)],
        out_specs=row(d),
        out_shape=jax.ShapeDtypeStruct((t, d), F32),
        compiler_params=_params(("parallel",)),
        name="merge_out",
    )(ya, yb, oc, od, proj, proj, proj, x2d, p2d, wa, wb, wc, wd, wo, wple, pleg, wpg)


def kernel(x, p, positions, norm_g, w_in, ln_v_g, ln_v_b, sgu_w, sgu_b, w_a_out, pool_w, pool_scale,
           w_b_out, cq_norm_g, w_uq, ckv_norm_g, w_ukv, q_norm_g, k_norm_g, w_c_out, w_d_out, w_o,
           w_ple, ple_norm_g, w_ple_gate):
    batch, seq, d = x.shape
    depth = w_in.shape[0]
    t = batch * seq
    x2d = x.reshape(t, d)
    p2d = p.reshape(depth * t, PLE_DIM)
    assert d == D_MODEL and seq % 1024 == 0 and t % 2048 == 0, (batch, seq, d)
    cos_l, sin_l = _rope_tables(positions)

    row = lambda g: g.reshape(depth, 1, -1)
    bf16 = lambda w: w.astype(BF16)
    dq_start, dq_size = _reference_layout()["dq"]
    in_scale = np.ones((1, 1, w_in.shape[-1]), np.float32)
    in_scale[..., dq_start:dq_start + dq_size] = HEAD_DIM ** -0.5 * math.log2(math.e)
    w_in_p = _take_columns(bf16(w_in * in_scale), _proj_segments())
    w_uq_p = _take_columns(bf16(w_uq), _uq_segments())
    qk_gain = lambda g: jnp.concatenate([g, g[:, MLA_NOPE:]], axis=1).reshape(depth, 1, MLA_QK_PAD)

    for i in range(depth):
        proj = _inproj(x2d, row(norm_g), w_in_p, i)
        q, k, v = _mla_prep(proj, cos_l, sin_l, i, row(cq_norm_g), w_uq_p, row(ckv_norm_g), bf16(w_ukv),
                            qk_gain(q_norm_g), qk_gain(k_norm_g))
        oc = _mla_attn(q, k, v, batch, seq)
        od = _sb_attn(proj, batch, seq)
        ya, yb = _local_mixers(proj, seq, i, row(ln_v_g), row(ln_v_b), sgu_w, jnp.swapaxes(sgu_b, 1, 2),
                               bf16(pool_w), row(pool_scale))
        x2d = _merge(ya, yb, oc, od, proj, x2d, p2d, i, bf16(w_a_out), bf16(w_b_out), bf16(w_c_out),
                     bf16(w_d_out), bf16(w_o), bf16(w_ple), row(ple_norm_g), bf16(w_ple_gate))
    return x2d.reshape(batch, seq, d)
```

```python
import functools
import math

import numpy as np
import jax
import jax.numpy as jnp
from jax import lax
from jax.experimental import pallas as pl
from jax.experimental.pallas import tpu as pltpu

F32 = jnp.float32
BF16 = jnp.bfloat16

EPS = 1e-6
NEG_INF = -1e30
CHUNK = 64
D_MODEL = 1024
PLE_DIM = 256
SGU_BLOCK = 128
GROUP_DIM = 128
N_GROUPS = 4
BRANCH_WIDTH = 512
POOL_WINDOWS = (2, 4, 8, 16)
POOL_HALO = 16
HEADS = 4
MLA_NOPE = 128
MLA_ROPE = 64
MLA_QK = MLA_NOPE + MLA_ROPE
MLA_Q_RANK = 256
MLA_KV_RANK = 128
MLA_QK_PAD = 256
HEAD_DIM = 128
ROPE_THETA = 10000.0

LANES = 128
VMEM_LIMIT = 56 * 1024 * 1024

OFF_GATES = 0
OFF_AU = 4096
OFF_AV = OFF_AU + 512
OFF_AZ = OFF_AV + 512
OFF_BX = OFF_AZ + 512
OFF_BZ = OFF_BX + 512
OFF_CZ = OFF_BZ + 512
OFF_DZ = OFF_CZ + 512
OFF_DQ = OFF_DZ + 512
OFF_DK = OFF_DQ + 512
OFF_DV = OFF_DK + 512
OFF_CQ = OFF_DV + 512
OFF_CKV = OFF_CQ + MLA_Q_RANK
OFF_CKR = OFF_CKV + MLA_KV_RANK
N_PROJ = OFF_CKR + 2 * MLA_ROPE

SB_UNDERFLOW_LOG2 = 152.0


def _reference_layout():
    sizes = (512, 512, 512, 512, 512, MLA_Q_RANK, MLA_KV_RANK, MLA_ROPE, 512, 512, 512, 512, 512, 4096)
    names = ("au", "av", "az", "bx", "bz", "cq", "ckv", "ckr", "cz", "dq", "dk", "dv", "dz", "gates")
    starts = np.cumsum((0,) + sizes[:-1])
    return {n: (int(s), int(z)) for n, s, z in zip(names, starts, sizes)}


def _proj_segments():
    layout = _reference_layout()
    order = ("gates", "au", "av", "az", "bx", "bz", "cz", "dz", "dq", "dk", "dv", "cq", "ckv", "ckr", "ckr")
    return [layout[n] for n in order]


def _uq_segments():
    nope = [(h * MLA_QK, MLA_NOPE) for h in range(HEADS)]
    rope = [(h * MLA_QK + MLA_NOPE, MLA_ROPE) for h in range(HEADS)]
    return nope + rope


def _take_columns(w, segments):
    return jnp.concatenate([w[..., s:s + n] for s, n in segments], axis=-1)


def _sigmoid(y):
    return 0.5 * jnp.tanh(0.5 * y) + 0.5


def _silu(z):
    return z * _sigmoid(z)


def _lane_tiles(x, n):
    return jnp.concatenate([x] * n, axis=1)


def _params(semantics):
    return pltpu.CompilerParams(dimension_semantics=semantics, vmem_limit_bytes=VMEM_LIMIT)


def _chunks(total, width):
    out, start = [], 0
    while start < total:
        size = min(width, total - start)
        out.append((start, size))
        start += size
    return out


def _stage_w_in_kernel(w_ref, o_ref):
    dq_start, _ = _reference_layout()["dq"]
    off = 0
    for start, size in _proj_segments():
        piece = w_ref[:, start:start + size]
        if start == dq_start:
            piece = piece * (HEAD_DIM ** -0.5 * math.log2(math.e))
        o_ref[:, off:off + size] = piece.astype(BF16)
        off += size


def _stage_w_in(w_in):
    depth, d, n_in = w_in.shape
    tr = 128
    return pl.pallas_call(
        _stage_w_in_kernel,
        grid=(depth, d // tr),
        in_specs=[pl.BlockSpec((None, tr, n_in), lambda l, r: (l, r, 0))],
        out_specs=pl.BlockSpec((None, tr, N_PROJ), lambda l, r: (l, r, 0)),
        out_shape=jax.ShapeDtypeStruct((depth, d, N_PROJ), BF16),
        compiler_params=_params(("parallel", "parallel")),
        name="stage_w_in",
    )(w_in)


def _inproj_kernel(x_ref, g_ref, w_ref, o_ref, h_sc):
    @pl.when(pl.program_id(1) == 0)
    def _():
        x = x_ref[...]
        ms = jnp.mean(x * x, axis=-1, keepdims=True)
        h_sc[...] = (x * lax.rsqrt(ms + EPS) * g_ref[...]).astype(BF16)

    h = h_sc[...]
    for start, size in _chunks(o_ref.shape[1], 512):
        cols = slice(start, start + size)
        o_ref[:, cols] = jnp.dot(h, w_ref[:, cols], preferred_element_type=F32).astype(BF16)


def _layer_spec(stacked, layer, single_buffer=False):
    shape = stacked.shape[1:]
    mode = pl.Buffered(1) if single_buffer else None
    return pl.BlockSpec((None,) + shape, lambda *_: (layer,) + (0,) * len(shape), pipeline_mode=mode)


def _inproj(x2d, g, w, layer):
    t, d = x2d.shape
    tm, tn = 1024, N_PROJ // 2
    return pl.pallas_call(
        _inproj_kernel,
        grid=(t // tm, N_PROJ // tn),
        in_specs=[
            pl.BlockSpec((tm, d), lambda i, j: (i, 0)),
            _layer_spec(g, layer),
            pl.BlockSpec((None, d, tn), lambda i, j: (layer, 0, j)),
        ],
        out_specs=pl.BlockSpec((tm, tn), lambda i, j: (i, j)),
        out_shape=jax.ShapeDtypeStruct((t, N_PROJ), BF16),
        scratch_shapes=[pltpu.VMEM((tm, d), BF16)],
        compiler_params=_params(("parallel", "arbitrary")),
        name="inproj",
    )(x2d, g, w)


def _rope_table_kernel(pos_ref, freq_ref, sign_ref, cos_ref, sin_ref):
    ang = freq_ref[...] * pos_ref[...].astype(F32)
    reps = LANES // ang.shape[0]
    cos_ref[...] = jnp.concatenate([jnp.cos(ang)] * reps, axis=0).T
    sin_ref[...] = (jnp.concatenate([jnp.sin(ang)] * reps, axis=0) * sign_ref[...]).T


def _rope_tables(positions):
    t = positions.size
    half = MLA_ROPE // 2
    tt = 2048
    freqs = (ROPE_THETA ** (-jnp.arange(half, dtype=F32) / half)).reshape(half, 1)
    sign = np.where((np.arange(LANES) % MLA_ROPE) < half, -1.0, 1.0).astype(np.float32).reshape(LANES, 1)
    return pl.pallas_call(
        _rope_table_kernel,
        grid=(t // tt,),
        in_specs=[pl.BlockSpec((1, tt), lambda i: (0, i)),
                  pl.BlockSpec((half, 1), lambda i: (0, 0)),
                  pl.BlockSpec((LANES, 1), lambda i: (0, 0))],
        out_specs=[pl.BlockSpec((tt, LANES), lambda i: (i, 0)),
                   pl.BlockSpec((tt, LANES), lambda i: (i, 0))],
        out_shape=[jax.ShapeDtypeStruct((t, LANES), F32)] * 2,
        compiler_params=_params(("parallel",)),
        name="rope_tables",
    )(positions.reshape(1, t), freqs, sign)


def _mla_prep_kernel(cq_ref, ckv_ref, ckr_ref, cos_ref, sin_ref, cqg_ref, wuq_ref, ckvg_ref,
                     wukv_ref, qng_ref, kng_ref, q_ref, k_ref, v_ref):
    cos = cos_ref[...]
    sin = sin_ref[...]
    lane = lax.broadcasted_iota(jnp.int32, (1, LANES), 1)
    first_half = (lane % MLA_ROPE) < (MLA_ROPE // 2)
    low_group = lane < MLA_ROPE
    inv_qk = 1.0 / MLA_QK
    scale = MLA_QK ** -0.5 * math.log2(math.e)
    ones_column = jnp.where(lane == 0, 1.0, 0.0).astype(BF16)

    def rotary(xr):
        partner = jnp.where(first_half, pltpu.roll(xr, LANES - MLA_ROPE // 2, 1),
                            pltpu.roll(xr, MLA_ROPE // 2, 1))
        return xr * cos + partner * sin

    def rms(xf, g):
        return xf * lax.rsqrt(jnp.mean(xf * xf, axis=-1, keepdims=True) + EPS) * g

    qng = qng_ref[...]
    kng = kng_ref[...]
    q_raw = jnp.dot(rms(cq_ref[...].astype(F32), cqg_ref[...]).astype(BF16), wuq_ref[...],
                    preferred_element_type=F32)
    kv_raw = jnp.dot(rms(ckv_ref[...].astype(F32), ckvg_ref[...]).astype(BF16), wukv_ref[...],
                     preferred_element_type=F32)
    ckr = ckr_ref[...].astype(F32)
    ts = ckr.shape[0]

    ss_kr = 0.5 * jnp.sum(ckr * ckr, axis=-1, keepdims=True)
    kr_rot = rotary(ckr * kng[:, MLA_NOPE:])
    for h in range(HEADS):
        kn = kv_raw[:, 2 * h * HEAD_DIM:(2 * h + 1) * HEAD_DIM]
        r = lax.rsqrt((jnp.sum(kn * kn, axis=-1, keepdims=True) + ss_kr) * inv_qk + EPS)
        mine = low_group if h % 2 == 0 else jnp.logical_not(low_group)
        k_ref[h, :, 0:MLA_NOPE] = (kn * r * kng[:, :MLA_NOPE]).astype(BF16)
        k_ref[h, :, MLA_NOPE:] = jnp.where(mine, kr_rot * r, 0.0).astype(BF16)
        v_ref[h, :, 0:HEAD_DIM] = kv_raw[:, (2 * h + 1) * HEAD_DIM:(2 * h + 2) * HEAD_DIM].astype(BF16)
        v_ref[h, :, HEAD_DIM:] = jnp.broadcast_to(ones_column, (ts, LANES))

    for j in range(HEADS // 2):
        blk = q_raw[:, HEADS * MLA_NOPE + j * LANES:HEADS * MLA_NOPE + (j + 1) * LANES]
        sq = blk * blk
        rs = []
        for h in (2 * j, 2 * j + 1):
            mine = low_group if h % 2 == 0 else jnp.logical_not(low_group)
            qn = q_raw[:, h * MLA_NOPE:(h + 1) * MLA_NOPE]
            ss = jnp.sum(qn * qn, axis=-1, keepdims=True) + jnp.sum(
                jnp.where(mine, sq, 0.0), axis=-1, keepdims=True)
            r = lax.rsqrt(ss * inv_qk + EPS)
            rs.append(r)
            q_ref[h, :, 0:MLA_NOPE] = (qn * r * qng[:, :MLA_NOPE] * scale).astype(BF16)
        rot = rotary(blk * jnp.where(low_group, rs[0], rs[1]) * qng[:, MLA_NOPE:]) * scale
        q_ref[2 * j, :, MLA_NOPE:] = jnp.where(low_group, rot, 0.0).astype(BF16)
        q_ref[2 * j + 1, :, MLA_NOPE:] = jnp.where(low_group, 0.0, rot).astype(BF16)


def _mla_prep(proj, cos_l, sin_l, layer, cqg, wuq, ckvg, wukv, qng, kng):
    t = proj.shape[0]
    ts = 512
    head_major = pl.BlockSpec((HEADS, ts, MLA_QK_PAD), lambda i: (0, i, 0))
    return pl.pallas_call(
        _mla_prep_kernel,
        grid=(t // ts,),
        in_specs=[
            pl.BlockSpec((ts, MLA_Q_RANK), lambda i: (i, OFF_CQ // MLA_Q_RANK)),
            pl.BlockSpec((ts, MLA_KV_RANK), lambda i: (i, OFF_CKV // MLA_KV_RANK)),
            pl.BlockSpec((ts, LANES), lambda i: (i, OFF_CKR // LANES)),
            pl.BlockSpec((ts, LANES), lambda i: (i, 0)),
            pl.BlockSpec((ts, LANES), lambda i: (i, 0)),
        ] + [_layer_spec(a, layer) for a in (cqg, wuq, ckvg, wukv, qng, kng)],
        out_specs=[head_major] * 3,
        out_shape=[jax.ShapeDtypeStruct((HEADS, t, MLA_QK_PAD), BF16)] * 3,
        compiler_params=_params(("parallel",)),
        name="mla_prep",
    )(proj, proj, proj, cos_l, sin_l, cqg, wuq, ckvg, wukv, qng, kng)


def _mla_attn_kernel(q_ref, k_ref, v_ref, o_ref, m_sc, acc_sc, sa_sc, sb_sc, *, tq):
    i = pl.program_id(2)
    tk = tq // 2
    everything = slice(0, tq)
    upper, lower = slice(0, tk), slice(tk, tq)

    def block_rows(b):
        return pl.ds(pl.multiple_of(b * tk, tk), tk)

    def scores(b, s_ref, rows=everything):
        s_ref[rows, :] = lax.dot_general(q_ref[rows, :], k_ref[block_rows(b), :],
                                         (((1,), (1,)), ((), ())), preferred_element_type=F32)

    def mask_diagonal(s_ref, rows):
        qc = lax.broadcasted_iota(jnp.int32, (tk, tk), 0) // CHUNK
        kc = lax.broadcasted_iota(jnp.int32, (tk, tk), 1) // CHUNK
        s_ref[rows, :] = jnp.where(kc <= qc, s_ref[rows, :], NEG_INF)

    def consume(b, s_ref, rows=everything):
        m_prev = m_sc[rows, :]
        m_new = jnp.maximum(m_prev, jnp.max(s_ref[rows, :], axis=-1, keepdims=True))
        p = jnp.exp2(s_ref[rows, :] - _lane_tiles(m_new, tk // LANES)).astype(BF16)
        alpha = _lane_tiles(jnp.exp2(m_prev - m_new), MLA_QK_PAD // LANES)
        acc_sc[rows, :] = (alpha * acc_sc[rows, :]
                           + jnp.dot(p, v_ref[block_rows(b), :], preferred_element_type=F32))
        m_sc[rows, :] = m_new

    m_sc[...] = jnp.full_like(m_sc, -jnp.inf)
    acc_sc[...] = jnp.zeros_like(acc_sc)
    scores(0, sa_sc, upper)
    scores(0, sa_sc, lower)

    def pairs(first_block, n_pairs):
        for n in range(n_pairs):
            b = first_block + 2 * n
            scores(b + 1, sb_sc)
            consume(b, sa_sc)
            scores(b + 2, sa_sc)
            consume(b + 1, sb_sc)

    def quad(jj, carry):
        pairs(4 * jj, 2)
        return carry

    lax.fori_loop(0, i // 2, quad, 0)

    @pl.when(i % 2 == 1)
    def _():
        pairs(2 * (i - 1), 1)

    scores(2 * i + 1, sb_sc, lower)
    mask_diagonal(sa_sc, upper)
    consume(2 * i, sa_sc)
    mask_diagonal(sb_sc, lower)
    consume(2 * i + 1, sb_sc, lower)

    acc = acc_sc[...]
    o_ref[...] = (acc[:, :HEAD_DIM] / acc[:, HEAD_DIM:HEAD_DIM + 1]).astype(BF16)


def _mla_attn(q, k, v, batch, seq):
    tq = 1024
    nq = seq // tq
    return pl.pallas_call(
        functools.partial(_mla_attn_kernel, tq=tq),
        grid=(batch, HEADS, nq),
        in_specs=[
            pl.BlockSpec((None, tq, MLA_QK_PAD), lambda b, h, i: (h, b * nq + i, 0)),
            pl.BlockSpec((None, seq, MLA_QK_PAD), lambda b, h, i: (h, b, 0)),
            pl.BlockSpec((None, seq, MLA_QK_PAD), lambda b, h, i: (h, b, 0)),
        ],
        out_specs=pl.BlockSpec((tq, HEAD_DIM), lambda b, h, i: (b * nq + i, h)),
        out_shape=jax.ShapeDtypeStruct((batch * seq, BRANCH_WIDTH), BF16),
        scratch_shapes=[pltpu.VMEM((tq, LANES), F32), pltpu.VMEM((tq, MLA_QK_PAD), F32),
                        pltpu.VMEM((tq, tq // 2), F32), pltpu.VMEM((tq, tq // 2), F32)],
        compiler_params=_params(("parallel", "parallel", "arbitrary")),
        name="mla_attn",
    )(q, k, v)


def _sb_attn_kernel(q_ref, k_ref, v_ref, o_ref, acc_sc, run_sc, low_sm, *, tq):
    i = pl.program_id(1)
    row = lax.broadcasted_iota(jnp.int32, (tq, tq), 0)
    col = lax.broadcasted_iota(jnp.int32, (tq, tq), 1)
    tri = jnp.where(row >= col, 1.0, 0.0).astype(BF16)
    strict = col < row
    sign_bit = jnp.uint32(0x80000000)

    def tiles(specs):
        chains = [(pl.ds(pl.multiple_of(j * tq, tq), tq), masked, slice(h * HEAD_DIM, (h + 1) * HEAD_DIM), h)
                  for j, masked in specs for h in range(HEADS)]
        z = [lax.dot_general(q_ref[:, cols], k_ref[rows, cols], (((1,), (1,)), ((), ())),
                             preferred_element_type=F32) for rows, _, cols, _ in chains]
        pen16, total = [], []
        for n, (_, masked, _, _) in enumerate(chains):
            neg_abs = lax.bitcast_convert_type(lax.bitcast_convert_type(z[n], jnp.uint32) | sign_bit, F32)
            pen = jnp.maximum(z[n], 0.0) + jnp.log2(1.0 + jnp.exp2(neg_abs))
            if masked:
                pen = jnp.where(strict, pen, 0.0)
            pen16.append(pen.astype(BF16))
            total.append(jnp.sum(pen, axis=-1, keepdims=True))
        c = [jnp.dot(s, tri, preferred_element_type=F32) for s in pen16]
        run = [run_sc[h] for h in range(HEADS)]
        a = []
        for n, (_, masked, _, h) in enumerate(chains):
            an = jnp.exp2(z[n] - c[n] - _lane_tiles(run[h], tq // LANES))
            if masked:
                an = jnp.where(strict, an, 0.0)
            a.append(an.astype(BF16))
            run[h] = run[h] + total[n]
        pv = [jnp.dot(a[n], v_ref[rows, cols], preferred_element_type=F32)
              for n, (rows, _, cols, _) in enumerate(chains)]
        for n, (_, _, cols, _) in enumerate(chains):
            acc_sc[:, cols] += pv[n]
        low = None
        for h in range(HEADS):
            run_sc[h] = run[h]
            low = run[h] if low is None else jnp.minimum(low, run[h])
        return jnp.min(low)

    acc_sc[...] = jnp.zeros_like(acc_sc)
    run_sc[...] = jnp.zeros_like(run_sc)

    @pl.when(i == 0)
    def _():
        low_sm[0] = tiles([(i, True)])

    @pl.when(i > 0)
    def _():
        low_sm[0] = tiles([(i, True), (i - 1, False)])

    def cond(state):
        j, low = state
        return jnp.logical_and(j >= 0, low < SB_UNDERFLOW_LOG2)

    def body(state):
        j, _ = state
        return j - 1, tiles([(j, False)])

    lax.while_loop(cond, body, (i - 2, low_sm[0]))
    o_ref[...] = acc_sc[...].astype(BF16)


def _sb_attn(proj, batch, seq):
    tq = 256
    nq = seq // tq
    w = BRANCH_WIDTH
    resident = lambda off: pl.BlockSpec((seq, w), lambda b, i: (b, off // w),
                                        pipeline_mode=pl.Buffered(1))
    return pl.pallas_call(
        functools.partial(_sb_attn_kernel, tq=tq),
        grid=(batch, nq),
        in_specs=[
            pl.BlockSpec((tq, w), lambda b, i: (b * nq + i, OFF_DQ // w)),
            resident(OFF_DK),
            resident(OFF_DV),
        ],
        out_specs=pl.BlockSpec((tq, w), lambda b, i: (b * nq + i, 0)),
        out_shape=jax.ShapeDtypeStruct((batch * seq, w), BF16),
        scratch_shapes=[pltpu.VMEM((tq, w), F32), pltpu.VMEM((HEADS, tq, LANES), F32),
                        pltpu.SMEM((1,), F32)],
        compiler_params=_params(("parallel", "arbitrary")),
        name="sb_attn",
    )(proj, proj, proj)


def _local_kernel(au_ref, av_ref, az_ref, bx_ref, halo_ref, bz_ref, lng_ref, lnb_ref, sw_ref, sb_ref,
                  pw_ref, ps_ref, ya_ref, yb_ref, *, tm, seq):
    i = pl.program_id(0)

    v = av_ref[...].astype(F32)
    mu = jnp.mean(v, axis=-1, keepdims=True)
    vc = v - mu
    vn = (vc * lax.rsqrt(jnp.mean(vc * vc, axis=-1, keepdims=True) + EPS) * lng_ref[...]
          + lnb_ref[...]).astype(BF16)
    tpos = lax.broadcasted_iota(jnp.int32, (SGU_BLOCK, SGU_BLOCK), 0) // CHUNK
    spos = lax.broadcasted_iota(jnp.int32, (SGU_BLOCK, SGU_BLOCK), 1) // CHUNK
    bias_t = sb_ref[...]
    for g in range(N_GROUPS):
        w = jnp.where(spos <= tpos, sw_ref[g], 0.0).astype(BF16)
        cols = slice(g * GROUP_DIM, (g + 1) * GROUP_DIM)
        for n in range(tm // SGU_BLOCK):
            rows = slice(n * SGU_BLOCK, (n + 1) * SGU_BLOCK)
            mixed = jnp.dot(w, vn[rows, cols], preferred_element_type=F32) + bias_t[:, g:g + 1]
            ya_ref[rows, cols] = (au_ref[rows, cols].astype(F32) * mixed
                                  * _silu(az_ref[rows, cols].astype(F32))).astype(BF16)

    start = (i * tm) % seq
    halo = jnp.where(start == 0, 0.0, halo_ref[...].astype(F32))
    xfull = jnp.concatenate([halo, bx_ref[...].astype(F32)], axis=0)
    pos = start + lax.broadcasted_iota(jnp.int32, (tm, 1), 0)
    for g, win in enumerate(POOL_WINDOWS):
        cols = slice(g * GROUP_DIM, (g + 1) * GROUP_DIM)
        xg = xfull[:, cols]
        s, shift = xg, 1
        while shift < win:
            s = s + pltpu.roll(s, shift, 0)
            shift *= 2
        count = jnp.minimum(pos + 1, win).astype(F32)
        pooled = s[POOL_HALO:] / count - xg[POOL_HALO:]
        y = jnp.dot(pooled.astype(BF16), pw_ref[g], preferred_element_type=F32) * ps_ref[:, cols]
        yb_ref[:, cols] = (y * _silu(bz_ref[:, cols].astype(F32))).astype(BF16)


def _mix_merge_kernel(au_ref, av_ref, az_ref, bx_ref, halo_ref, bz_ref, oc_ref, od_ref, cz_ref, dz_ref,
                      gates_ref, x_ref, p_ref, lng_ref, lnb_ref, sw_ref, sb_ref, pw_ref, ps_ref,
                      wa_ref, wb_ref, wc_ref, wd_ref, wo_ref, wple_ref, pleg_ref, wpg_ref, o_ref,
                      ya_sc, yb_sc, *, tm, seq):
    def gated(n, y, w_ref):
        gate = _sigmoid(gates_ref[:, n * D_MODEL:(n + 1) * D_MODEL])
        return gate.astype(F32) * jnp.dot(y, w_ref[...], preferred_element_type=F32)

    yc = (oc_ref[...].astype(F32) * _silu(cz_ref[...].astype(F32))).astype(BF16)
    yd = (od_ref[...].astype(F32) * _silu(dz_ref[...].astype(F32))).astype(BF16)
    merged = gated(2, yc, wc_ref) + gated(3, yd, wd_ref)
    e = jnp.dot(p_ref[...].astype(BF16), wple_ref[...], preferred_element_type=F32)

    _local_kernel(au_ref, av_ref, az_ref, bx_ref, halo_ref, bz_ref, lng_ref, lnb_ref, sw_ref, sb_ref,
                  pw_ref, ps_ref, ya_sc, yb_sc, tm=tm, seq=seq)

    merged = merged + gated(0, ya_sc[...], wa_ref) + gated(1, yb_sc[...], wb_ref)
    x1 = x_ref[...] + jnp.dot(merged.astype(BF16), wo_ref[...], preferred_element_type=F32)
    n1 = (x1 * lax.rsqrt(jnp.mean(x1 * x1, axis=-1, keepdims=True) + EPS) * pleg_ref[...]).astype(BF16)
    o_ref[...] = x1 + _sigmoid(jnp.dot(n1, wpg_ref[...], preferred_element_type=F32)) * e


def _mix_merge(oc, od, proj, x2d, p2d, seq, layer, lng, lnb, sgu_w, sgu_bt, pool_w, pool_scale,
               wa, wb, wc, wd, wo, wple, pleg, wpg):
    t, d = x2d.shape
    tm = 512
    w = BRANCH_WIDTH
    row = lambda width: pl.BlockSpec((tm, width), lambda i: (i, 0))
    col = lambda off: pl.BlockSpec((tm, w), lambda i: (i, off // w))
    halo = pl.BlockSpec((POOL_HALO, w),
                        lambda i: (jnp.maximum(i * (tm // POOL_HALO) - 1, 0), OFF_BX // w))
    params = (lng, lnb, sgu_w, sgu_bt, pool_w, pool_scale, wa, wb, wc, wd, wo, wple, pleg, wpg)
    return pl.pallas_call(
        functools.partial(_mix_merge_kernel, tm=tm, seq=seq),
        grid=(t // tm,),
        in_specs=[col(OFF_AU), col(OFF_AV), col(OFF_AZ), col(OFF_BX), halo, col(OFF_BZ),
                  row(w), row(w), col(OFF_CZ), col(OFF_DZ),
                  pl.BlockSpec((tm, 4 * d), lambda i: (i, OFF_GATES // (4 * d))),
                  row(d),
                  pl.BlockSpec((tm, PLE_DIM), lambda i: (layer * (t // tm) + i, 0))]
        + [_layer_spec(a, layer, single_buffer=True) for a in params],
        out_specs=row(d),
        out_shape=jax.ShapeDtypeStruct((t, d), F32),
        scratch_shapes=[pltpu.VMEM((tm, w), BF16), pltpu.VMEM((tm, w), BF16)],
        compiler_params=_params(("parallel",)),
        name="mix_merge",
    )(proj, proj, proj, proj, proj, proj, oc, od, proj, proj, proj, x2d, p2d, *params)


def kernel(x, p, positions, norm_g, w_in, ln_v_g, ln_v_b, sgu_w, sgu_b, w_a_out, pool_w, pool_scale,
           w_b_out, cq_norm_g, w_uq, ckv_norm_g, w_ukv, q_norm_g, k_norm_g, w_c_out, w_d_out, w_o,
           w_ple, ple_norm_g, w_ple_gate):
    batch, seq, d = x.shape
    depth = w_in.shape[0]
    t = batch * seq
    x2d = x.reshape(t, d)
    p2d = p.reshape(depth * t, PLE_DIM)
    assert d == D_MODEL and seq % 1024 == 0 and t % 2048 == 0, (batch, seq, d)
    cos_l, sin_l = _rope_tables(positions)

    row = lambda g: g.reshape(depth, 1, -1)
    bf16 = lambda w: w.astype(BF16)
    w_in_p = _stage_w_in(w_in)
    w_uq_p = _take_columns(bf16(w_uq), _uq_segments())
    qk_gain = lambda g: jnp.concatenate([g, g[:, MLA_NOPE:]], axis=1).reshape(depth, 1, MLA_QK_PAD)

    for i in range(depth):
        proj = _inproj(x2d, row(norm_g), w_in_p, i)
        q, k, v = _mla_prep(proj, cos_l, sin_l, i, row(cq_norm_g), w_uq_p, row(ckv_norm_g), bf16(w_ukv),
                            qk_gain(q_norm_g), qk_gain(k_norm_g))
        oc = _mla_attn(q, k, v, batch, seq)
        od = _sb_attn(proj, batch, seq)
        x2d = _mix_merge(oc, od, proj, x2d, p2d, seq, i, row(ln_v_g), row(ln_v_b), sgu_w,
                         jnp.swapaxes(sgu_b, 1, 2), bf16(pool_w), row(pool_scale), bf16(w_a_out),
                         bf16(w_b_out), bf16(w_c_out), bf16(w_d_out), bf16(w_o), bf16(w_ple),
                         row(ple_norm_g), bf16(w_ple_gate))
    return x2d.reshape(batch, seq, d)
```

```python
import functools
import math

import numpy as np
import jax
import jax.numpy as jnp
from jax import lax
from jax.experimental import pallas as pl
from jax.experimental.pallas import tpu as pltpu

F32 = jnp.float32
BF16 = jnp.bfloat16

EPS = 1e-6
NEG_INF = -1e30
CHUNK = 64
D_MODEL = 1024
PLE_DIM = 256
SGU_BLOCK = 128
GROUP_DIM = 128
N_GROUPS = 4
BRANCH_WIDTH = 512
POOL_WINDOWS = (2, 4, 8, 16)
POOL_HALO = 16
HEADS = 4
MLA_NOPE = 128
MLA_ROPE = 64
MLA_QK = MLA_NOPE + MLA_ROPE
MLA_Q_RANK = 256
MLA_KV_RANK = 128
MLA_QK_PAD = 256
HEAD_DIM = 128
ROPE_THETA = 10000.0

LANES = 128
VMEM_LIMIT = 56 * 1024 * 1024

OFF_GATES = 0
OFF_AU = 4096
OFF_AV = OFF_AU + 512
OFF_AZ = OFF_AV + 512
OFF_BX = OFF_AZ + 512
OFF_BZ = OFF_BX + 512
OFF_CZ = OFF_BZ + 512
OFF_DZ = OFF_CZ + 512
OFF_DQ = OFF_DZ + 512
OFF_DK = OFF_DQ + 512
OFF_DV = OFF_DK + 512
OFF_CQ = OFF_DV + 512
OFF_CKV = OFF_CQ + MLA_Q_RANK
OFF_CKR = OFF_CKV + MLA_KV_RANK
N_PROJ = OFF_CKR + 2 * MLA_ROPE

SB_UNDERFLOW_LOG2 = 152.0


def _reference_layout():
    sizes = (512, 512, 512, 512, 512, MLA_Q_RANK, MLA_KV_RANK, MLA_ROPE, 512, 512, 512, 512, 512, 4096)
    names = ("au", "av", "az", "bx", "bz", "cq", "ckv", "ckr", "cz", "dq", "dk", "dv", "dz", "gates")
    starts = np.cumsum((0,) + sizes[:-1])
    return {n: (int(s), int(z)) for n, s, z in zip(names, starts, sizes)}


def _proj_segments():
    layout = _reference_layout()
    order = ("gates", "au", "av", "az", "bx", "bz", "cz", "dz", "dq", "dk", "dv", "cq", "ckv", "ckr", "ckr")
    return [layout[n] for n in order]


def _uq_segments():
    nope = [(h * MLA_QK, MLA_NOPE) for h in range(HEADS)]
    rope = [(h * MLA_QK + MLA_NOPE, MLA_ROPE) for h in range(HEADS)]
    return nope + rope


def _take_columns(w, segments):
    return jnp.concatenate([w[..., s:s + n] for s, n in segments], axis=-1)


def _sigmoid(y):
    return 0.5 * jnp.tanh(0.5 * y) + 0.5


def _silu(z):
    return z * _sigmoid(z)


def _lane_tiles(x, n):
    return jnp.concatenate([x] * n, axis=1)


def _params(semantics):
    return pltpu.CompilerParams(dimension_semantics=semantics, vmem_limit_bytes=VMEM_LIMIT)


def _chunks(total, width):
    out, start = [], 0
    while start < total:
        size = min(width, total - start)
        out.append((start, size))
        start += size
    return out


def _stage_w_in_kernel(wt_ref, o_ref):
    dq_start, _ = _reference_layout()["dq"]
    ckr = _reference_layout()["ckr"]
    off = 0
    for start, size in _proj_segments():
        if (start, size) == ckr:
            if off != OFF_CKR:
                continue
            piece = jnp.concatenate([wt_ref[start:start + size, :]] * 2, axis=0)
        else:
            piece = wt_ref[start:start + size, :]
        if start == dq_start:
            piece = piece * (HEAD_DIM ** -0.5 * math.log2(math.e))
        rows = piece.shape[0]
        o_ref[:, off:off + rows] = piece.T.astype(BF16)
        off += rows


def _stage_w_in(w_in):
    depth, d, n_in = w_in.shape
    td = 128
    return pl.pallas_call(
        _stage_w_in_kernel,
        grid=(depth, d // td),
        in_specs=[pl.BlockSpec((None, n_in, td), lambda l, r: (l, 0, r))],
        out_specs=pl.BlockSpec((None, td, N_PROJ), lambda l, r: (l, r, 0)),
        out_shape=jax.ShapeDtypeStruct((depth, d, N_PROJ), BF16),
        compiler_params=_params(("parallel", "parallel")),
        name="stage_w_in",
    )(jnp.swapaxes(w_in, 1, 2))


def _inproj_kernel(x_ref, g_ref, w_ref, o_ref):
    x = x_ref[...]
    ms = jnp.mean(x * x, axis=-1, keepdims=True)
    h = (x * lax.rsqrt(ms + EPS) * g_ref[...]).astype(BF16)
    for start, size in _chunks(o_ref.shape[1], 512):
        cols = slice(start, start + size)
        o_ref[:, cols] = jnp.dot(h, w_ref[:, cols], preferred_element_type=F32).astype(BF16)


def _layer_spec(stacked, layer, single_buffer=False):
    shape = stacked.shape[1:]
    mode = pl.Buffered(1) if single_buffer else None
    return pl.BlockSpec((None,) + shape, lambda *_: (layer,) + (0,) * len(shape), pipeline_mode=mode)


def _inproj(x2d, g, w, layer):
    t, d = x2d.shape
    tm, tn = 1024, N_PROJ // 2
    return pl.pallas_call(
        _inproj_kernel,
        grid=(t // tm, N_PROJ // tn),
        in_specs=[
            pl.BlockSpec((tm, d), lambda i, j: (i, 0)),
            _layer_spec(g, layer),
            pl.BlockSpec((None, d, tn), lambda i, j: (layer, 0, j)),
        ],
        out_specs=pl.BlockSpec((tm, tn), lambda i, j: (i, j)),
        out_shape=jax.ShapeDtypeStruct((t, N_PROJ), BF16),
        compiler_params=_params(("parallel", "arbitrary")),
        name="inproj",
    )(x2d, g, w)


def _rope_table_kernel(pos_ref, freq_ref, sign_ref, cos_ref, sin_ref):
    ang = freq_ref[...] * pos_ref[...].astype(F32)
    reps = LANES // ang.shape[0]
    cos_ref[...] = jnp.concatenate([jnp.cos(ang)] * reps, axis=0).T
    sin_ref[...] = (jnp.concatenate([jnp.sin(ang)] * reps, axis=0) * sign_ref[...]).T


def _rope_tables(positions):
    t = positions.size
    half = MLA_ROPE // 2
    tt = 2048
    freqs = (ROPE_THETA ** (-jnp.arange(half, dtype=F32) / half)).reshape(half, 1)
    sign = np.where((np.arange(LANES) % MLA_ROPE) < half, -1.0, 1.0).astype(np.float32).reshape(LANES, 1)
    return pl.pallas_call(
        _rope_table_kernel,
        grid=(t // tt,),
        in_specs=[pl.BlockSpec((1, tt), lambda i: (0, i)),
                  pl.BlockSpec((half, 1), lambda i: (0, 0)),
                  pl.BlockSpec((LANES, 1), lambda i: (0, 0))],
        out_specs=[pl.BlockSpec((tt, LANES), lambda i: (i, 0)),
                   pl.BlockSpec((tt, LANES), lambda i: (i, 0))],
        out_shape=[jax.ShapeDtypeStruct((t, LANES), F32)] * 2,
        compiler_params=_params(("parallel",)),
        name="rope_tables",
    )(positions.reshape(1, t), freqs, sign)


def _mla_prep_kernel(cq_ref, ckv_ref, ckr_ref, cos_ref, sin_ref, cqg_ref, wuq_ref, ckvg_ref,
                     wukv_ref, qng_ref, kng_ref, q_ref, k_ref, v_ref):
    cos = cos_ref[...]
    sin = sin_ref[...]
    lane = lax.broadcasted_iota(jnp.int32, (1, LANES), 1)
    first_half = (lane % MLA_ROPE) < (MLA_ROPE // 2)
    low_group = lane < MLA_ROPE
    inv_qk = 1.0 / MLA_QK
    scale = MLA_QK ** -0.5 * math.log2(math.e)

    def rotary(xr):
        partner = jnp.where(first_half, pltpu.roll(xr, LANES - MLA_ROPE // 2, 1),
                            pltpu.roll(xr, MLA_ROPE // 2, 1))
        return xr * cos + partner * sin

    def sum_squares(parts):
        sq = jnp.concatenate([(x * x).astype(BF16) for x, _ in parts], axis=1)
        wts = jnp.concatenate([w for _, w in parts], axis=0)
        return jnp.dot(sq, wts, preferred_element_type=F32)

    def rms(xf, g):
        ones = jnp.ones((xf.shape[1], LANES), BF16)
        r = lax.rsqrt(sum_squares([(xf, ones)]) * (1.0 / xf.shape[1]) + EPS)
        return xf * _lane_tiles(r, xf.shape[1] // LANES) * g

    ones_tile = jnp.ones((LANES, LANES), BF16)
    half_tile = jnp.full((LANES, LANES), 0.5, BF16)
    group_row = lax.broadcasted_iota(jnp.int32, (LANES, LANES), 0) < MLA_ROPE
    group_tile = (jnp.where(group_row, 1.0, 0.0).astype(BF16), jnp.where(group_row, 0.0, 1.0).astype(BF16))

    qng = qng_ref[...]
    kng = kng_ref[...]
    q_raw = jnp.dot(rms(cq_ref[...].astype(F32), cqg_ref[...]).astype(BF16), wuq_ref[...],
                    preferred_element_type=F32)
    kv_raw = jnp.dot(rms(ckv_ref[...].astype(F32), ckvg_ref[...]).astype(BF16), wukv_ref[...],
                     preferred_element_type=F32)
    ckr = ckr_ref[...].astype(F32)
    ts = ckr.shape[0]

    kr_rot = rotary(ckr * kng[:, MLA_NOPE:])
    for h in range(HEADS):
        kn = kv_raw[:, 2 * h * HEAD_DIM:(2 * h + 1) * HEAD_DIM]
        r = lax.rsqrt(sum_squares([(kn, ones_tile), (ckr, half_tile)]) * inv_qk + EPS)
        mine = low_group if h % 2 == 0 else jnp.logical_not(low_group)
        k_ref[h, :, 0:MLA_NOPE] = (kn * r * kng[:, :MLA_NOPE]).astype(BF16)
        k_ref[h, :, MLA_NOPE:] = jnp.where(mine, kr_rot * r, 0.0).astype(BF16)
        v_ref[h, :, 0:HEAD_DIM] = kv_raw[:, (2 * h + 1) * HEAD_DIM:(2 * h + 2) * HEAD_DIM].astype(BF16)
        v_ref[h, :, HEAD_DIM:] = jnp.ones((ts, LANES), BF16)

    for j in range(HEADS // 2):
        blk = q_raw[:, HEADS * MLA_NOPE + j * LANES:HEADS * MLA_NOPE + (j + 1) * LANES]
        rs = []
        for h in (2 * j, 2 * j + 1):
            qn = q_raw[:, h * MLA_NOPE:(h + 1) * MLA_NOPE]
            r = lax.rsqrt(sum_squares([(qn, ones_tile), (blk, group_tile[h % 2])]) * inv_qk + EPS)
            rs.append(r)
            q_ref[h, :, 0:MLA_NOPE] = (qn * r * (qng[:, :MLA_NOPE] * scale)).astype(BF16)
        rot = rotary(blk * jnp.where(low_group, rs[0], rs[1]) * (qng[:, MLA_NOPE:] * scale))
        q_ref[2 * j, :, MLA_NOPE:] = jnp.where(low_group, rot, 0.0).astype(BF16)
        q_ref[2 * j + 1, :, MLA_NOPE:] = jnp.where(low_group, 0.0, rot).astype(BF16)


def _mla_prep(proj, cos_l, sin_l, layer, cqg, wuq, ckvg, wukv, qng, kng):
    t = proj.shape[0]
    ts = 512
    head_major = pl.BlockSpec((HEADS, ts, MLA_QK_PAD), lambda i: (0, i, 0))
    return pl.pallas_call(
        _mla_prep_kernel,
        grid=(t // ts,),
        in_specs=[
            pl.BlockSpec((ts, MLA_Q_RANK), lambda i: (i, OFF_CQ // MLA_Q_RANK)),
            pl.BlockSpec((ts, MLA_KV_RANK), lambda i: (i, OFF_CKV // MLA_KV_RANK)),
            pl.BlockSpec((ts, LANES), lambda i: (i, OFF_CKR // LANES)),
            pl.BlockSpec((ts, LANES), lambda i: (i, 0)),
            pl.BlockSpec((ts, LANES), lambda i: (i, 0)),
        ] + [_layer_spec(a, layer) for a in (cqg, wuq, ckvg, wukv, qng, kng)],
        out_specs=[head_major] * 3,
        out_shape=[jax.ShapeDtypeStruct((HEADS, t, MLA_QK_PAD), BF16)] * 3,
        compiler_params=_params(("parallel",)),
        name="mla_prep",
    )(proj, proj, proj, cos_l, sin_l, cqg, wuq, ckvg, wukv, qng, kng)


def _mla_attn_kernel(q_ref, k_ref, v_ref, o_ref, m_sc, acc_sc, sa_sc, sb_sc, *, tq):
    i = pl.program_id(2)
    tk = tq // 2
    everything = slice(0, tq)
    upper, lower = slice(0, tk), slice(tk, tq)

    def block_rows(b):
        return pl.ds(pl.multiple_of(b * tk, tk), tk)

    def scores(b, s_ref, rows=everything):
        s_ref[rows, :] = lax.dot_general(q_ref[rows, :], k_ref[block_rows(b), :],
                                         (((1,), (1,)), ((), ())), preferred_element_type=F32)

    def mask_diagonal(s_ref, rows):
        for c in range(tk // CHUNK - 1):
            hidden = tk - (c + 1) * CHUNK
            s_ref[rows.start + c * CHUNK:rows.start + (c + 1) * CHUNK, tk - hidden:] = jnp.full(
                (CHUNK, hidden), NEG_INF, F32)

    def consume(b, s_ref, rows=everything):
        m_prev = m_sc[rows, :]
        m_new = jnp.maximum(m_prev, jnp.max(s_ref[rows, :], axis=-1, keepdims=True))
        p = jnp.exp2(s_ref[rows, :] - _lane_tiles(m_new, tk // LANES)).astype(BF16)
        alpha = _lane_tiles(jnp.exp2(m_prev - m_new), MLA_QK_PAD // LANES)
        acc_sc[rows, :] = (alpha * acc_sc[rows, :]
                           + jnp.dot(p, v_ref[block_rows(b), :], preferred_element_type=F32))
        m_sc[rows, :] = m_new

    m_sc[...] = jnp.full_like(m_sc, -jnp.inf)
    acc_sc[...] = jnp.zeros_like(acc_sc)
    scores(0, sa_sc, upper)
    scores(0, sa_sc, lower)

    def pairs(first_block, n_pairs):
        for n in range(n_pairs):
            b = first_block + 2 * n
            scores(b + 1, sb_sc)
            consume(b, sa_sc)
            scores(b + 2, sa_sc)
            consume(b + 1, sb_sc)

    def quad(jj, carry):
        pairs(4 * jj, 2)
        return carry

    lax.fori_loop(0, i // 2, quad, 0)

    @pl.when(i % 2 == 1)
    def _():
        pairs(2 * (i - 1), 1)

    scores(2 * i + 1, sb_sc, lower)
    mask_diagonal(sa_sc, upper)
    consume(2 * i, sa_sc)
    mask_diagonal(sb_sc, lower)
    consume(2 * i + 1, sb_sc, lower)

    acc = acc_sc[...]
    o_ref[...] = (acc[:, :HEAD_DIM] / acc[:, HEAD_DIM:]).astype(BF16)


def _mla_attn(q, k, v, batch, seq):
    tq = 1024
    nq = seq // tq
    return pl.pallas_call(
        functools.partial(_mla_attn_kernel, tq=tq),
        grid=(batch, HEADS, nq),
        in_specs=[
            pl.BlockSpec((None, tq, MLA_QK_PAD), lambda b, h, i: (h, b * nq + i, 0)),
            pl.BlockSpec((None, seq, MLA_QK_PAD), lambda b, h, i: (h, b, 0)),
            pl.BlockSpec((None, seq, MLA_QK_PAD), lambda b, h, i: (h, b, 0)),
        ],
        out_specs=pl.BlockSpec((tq, HEAD_DIM), lambda b, h, i: (b * nq + i, h)),
        out_shape=jax.ShapeDtypeStruct((batch * seq, BRANCH_WIDTH), BF16),
        scratch_shapes=[pltpu.VMEM((tq, LANES), F32), pltpu.VMEM((tq, MLA_QK_PAD), F32),
                        pltpu.VMEM((tq, tq // 2), F32), pltpu.VMEM((tq, tq // 2), F32)],
        compiler_params=_params(("parallel", "parallel", "arbitrary")),
        name="mla_attn",
    )(q, k, v)


def _sb_attn_kernel(q_ref, k_ref, v_ref, o_ref, acc_sc, run_sc, low_sm, *, tq):
    i = pl.program_id(1)
    row = lax.broadcasted_iota(jnp.int32, (tq, tq), 0)
    col = lax.broadcasted_iota(jnp.int32, (tq, tq), 1)
    tri = jnp.where(row >= col, 1.0, 0.0).astype(BF16)
    strict = col < row
    sign_bit = jnp.uint32(0x80000000)

    def tiles(specs):
        chains = [(pl.ds(pl.multiple_of(j * tq, tq), tq), masked, slice(h * HEAD_DIM, (h + 1) * HEAD_DIM), h)
                  for j, masked in specs for h in range(HEADS)]
        z = [lax.dot_general(q_ref[:, cols], k_ref[rows, cols], (((1,), (1,)), ((), ())),
                             preferred_element_type=F32) for rows, _, cols, _ in chains]
        pen16, total = [], []
        for n, (_, masked, _, _) in enumerate(chains):
            neg_abs = lax.bitcast_convert_type(lax.bitcast_convert_type(z[n], jnp.uint32) | sign_bit, F32)
            pen = jnp.maximum(z[n], 0.0) + jnp.log2(1.0 + jnp.exp2(neg_abs))
            if masked:
                pen = jnp.where(strict, pen, 0.0)
            pen16.append(pen.astype(BF16))
            total.append(jnp.sum(pen, axis=-1, keepdims=True))
        c = [jnp.dot(s, tri, preferred_element_type=F32) for s in pen16]
        run = [run_sc[h] for h in range(HEADS)]
        a = []
        for n, (_, masked, _, h) in enumerate(chains):
            an = jnp.exp2(z[n] - c[n] - _lane_tiles(run[h], tq // LANES))
            if masked:
                an = jnp.where(strict, an, 0.0)
            a.append(an.astype(BF16))
            run[h] = run[h] + total[n]
        pv = [jnp.dot(a[n], v_ref[rows, cols], preferred_element_type=F32)
              for n, (rows, _, cols, _) in enumerate(chains)]
        for n, (_, _, cols, _) in enumerate(chains):
            acc_sc[:, cols] += pv[n]
        low = None
        for h in range(HEADS):
            run_sc[h] = run[h]
            low = run[h] if low is None else jnp.minimum(low, run[h])
        return jnp.min(low)

    acc_sc[...] = jnp.zeros_like(acc_sc)
    run_sc[...] = jnp.zeros_like(run_sc)

    @pl.when(i == 0)
    def _():
        low_sm[0] = tiles([(i, True)])

    @pl.when(i > 0)
    def _():
        low_sm[0] = tiles([(i, True), (i - 1, False)])

    def cond(state):
        j, low = state
        return jnp.logical_and(j >= 0, low < SB_UNDERFLOW_LOG2)

    def body(state):
        j, _ = state
        return j - 1, tiles([(j, False)])

    lax.while_loop(cond, body, (i - 2, low_sm[0]))
    o_ref[...] = acc_sc[...].astype(BF16)


def _sb_attn(proj, batch, seq):
    tq = 256
    nq = seq // tq
    w = BRANCH_WIDTH
    resident = lambda off: pl.BlockSpec((seq, w), lambda b, i: (b, off // w),
                                        pipeline_mode=pl.Buffered(1))
    return pl.pallas_call(
        functools.partial(_sb_attn_kernel, tq=tq),
        grid=(batch, nq),
        in_specs=[
            pl.BlockSpec((tq, w), lambda b, i: (b * nq + i, OFF_DQ // w)),
            resident(OFF_DK),
            resident(OFF_DV),
        ],
        out_specs=pl.BlockSpec((tq, w), lambda b, i: (b * nq + i, 0)),
        out_shape=jax.ShapeDtypeStruct((batch * seq, w), BF16),
        scratch_shapes=[pltpu.VMEM((tq, w), F32), pltpu.VMEM((HEADS, tq, LANES), F32),
                        pltpu.SMEM((1,), F32)],
        compiler_params=_params(("parallel", "arbitrary")),
        name="sb_attn",
    )(proj, proj, proj)


def _local_kernel(au_ref, av_ref, az_ref, bx_ref, halo_ref, bz_ref, lng_ref, lnb_ref, sw_ref, sb_ref,
                  pw_ref, ps_ref, ya_ref, yb_ref, *, tm, seq):
    i = pl.program_id(0)

    v = av_ref[...].astype(F32)
    mu = jnp.mean(v, axis=-1, keepdims=True)
    vc = v - mu
    vn = (vc * lax.rsqrt(jnp.mean(vc * vc, axis=-1, keepdims=True) + EPS) * lng_ref[...]
          + lnb_ref[...]).astype(BF16)
    tpos = lax.broadcasted_iota(jnp.int32, (SGU_BLOCK, SGU_BLOCK), 0) // CHUNK
    spos = lax.broadcasted_iota(jnp.int32, (SGU_BLOCK, SGU_BLOCK), 1) // CHUNK
    bias_t = sb_ref[...]
    for g in range(N_GROUPS):
        w = jnp.where(spos <= tpos, sw_ref[g], 0.0).astype(BF16)
        cols = slice(g * GROUP_DIM, (g + 1) * GROUP_DIM)
        for n in range(tm // SGU_BLOCK):
            rows = slice(n * SGU_BLOCK, (n + 1) * SGU_BLOCK)
            mixed = jnp.dot(w, vn[rows, cols], preferred_element_type=F32) + bias_t[:, g:g + 1]
            ya_ref[rows, cols] = (au_ref[rows, cols].astype(F32) * mixed
                                  * _silu(az_ref[rows, cols].astype(F32))).astype(BF16)

    start = (i * tm) % seq
    halo = jnp.where(start == 0, 0.0, halo_ref[...].astype(F32))
    xfull = jnp.concatenate([halo, bx_ref[...].astype(F32)], axis=0)
    pos = start + lax.broadcasted_iota(jnp.int32, (tm, 1), 0)
    for g, win in enumerate(POOL_WINDOWS):
        cols = slice(g * GROUP_DIM, (g + 1) * GROUP_DIM)
        xg = xfull[:, cols]
        s, shift = xg, 1
        while shift < win:
            s = s + pltpu.roll(s, shift, 0)
            shift *= 2
        count = jnp.minimum(pos + 1, win).astype(F32)
        pooled = s[POOL_HALO:] / count - xg[POOL_HALO:]
        y = jnp.dot(pooled.astype(BF16), pw_ref[g], preferred_element_type=F32) * ps_ref[:, cols]
        yb_ref[:, cols] = (y * _silu(bz_ref[:, cols].astype(F32))).astype(BF16)


def _mix_merge_kernel(au_ref, av_ref, az_ref, bx_ref, halo_ref, bz_ref, oc_ref, od_ref, cz_ref, dz_ref,
                      gates_ref, x_ref, p_ref, lng_ref, lnb_ref, sw_ref, sb_ref, pw_ref, ps_ref,
                      wa_ref, wb_ref, wc_ref, wd_ref, wo_ref, wple_ref, pleg_ref, wpg_ref, o_ref,
                      ya_sc, yb_sc, *, tm, seq):
    def gated(n, y, w_ref):
        gate = _sigmoid(gates_ref[:, n * D_MODEL:(n + 1) * D_MODEL])
        return gate.astype(F32) * jnp.dot(y, w_ref[...], preferred_element_type=F32)

    yc = (oc_ref[...].astype(F32) * _silu(cz_ref[...].astype(F32))).astype(BF16)
    yd = (od_ref[...].astype(F32) * _silu(dz_ref[...].astype(F32))).astype(BF16)
    merged = gated(2, yc, wc_ref) + gated(3, yd, wd_ref)
    e = jnp.dot(p_ref[...].astype(BF16), wple_ref[...], preferred_element_type=F32)

    _local_kernel(au_ref, av_ref, az_ref, bx_ref, halo_ref, bz_ref, lng_ref, lnb_ref, sw_ref, sb_ref,
                  pw_ref, ps_ref, ya_sc, yb_sc, tm=tm, seq=seq)

    merged = merged + gated(0, ya_sc[...], wa_ref) + gated(1, yb_sc[...], wb_ref)
    x1 = x_ref[...] + jnp.dot(merged.astype(BF16), wo_ref[...], preferred_element_type=F32)
    n1 = (x1 * lax.rsqrt(jnp.mean(x1 * x1, axis=-1, keepdims=True) + EPS) * pleg_ref[...]).astype(BF16)
    o_ref[...] = x1 + _sigmoid(jnp.dot(n1, wpg_ref[...], preferred_element_type=F32)) * e


def _mix_merge(oc, od, proj, x2d, p2d, seq, layer, lng, lnb, sgu_w, sgu_bt, pool_w, pool_scale,
               wa, wb, wc, wd, wo, wple, pleg, wpg):
    t, d = x2d.shape
    tm = 512
    w = BRANCH_WIDTH
    row = lambda width: pl.BlockSpec((tm, width), lambda i: (i, 0))
    col = lambda off: pl.BlockSpec((tm, w), lambda i: (i, off // w))
    halo = pl.BlockSpec((POOL_HALO, w),
                        lambda i: (jnp.maximum(i * (tm // POOL_HALO) - 1, 0), OFF_BX // w))
    params = (lng, lnb, sgu_w, sgu_bt, pool_w, pool_scale, wa, wb, wc, wd, wo, wple, pleg, wpg)
    return pl.pallas_call(
        functools.partial(_mix_merge_kernel, tm=tm, seq=seq),
        grid=(t // tm,),
        in_specs=[col(OFF_AU), col(OFF_AV), col(OFF_AZ), col(OFF_BX), halo, col(OFF_BZ),
                  row(w), row(w), col(OFF_CZ), col(OFF_DZ),
                  pl.BlockSpec((tm, 4 * d), lambda i: (i, OFF_GATES // (4 * d))),
                  row(d),
                  pl.BlockSpec((tm, PLE_DIM), lambda i: (layer * (t // tm) + i, 0))]
        + [_layer_spec(a, layer, single_buffer=True) for a in params],
        out_specs=row(d),
        out_shape=jax.ShapeDtypeStruct((t, d), F32),
        scratch_shapes=[pltpu.VMEM((tm, w), BF16), pltpu.VMEM((tm, w), BF16)],
        compiler_params=_params(("parallel",)),
        name="mix_merge",
    )(proj, proj, proj, proj, proj, proj, oc, od, proj, proj, proj, x2d, p2d, *params)


def kernel(x, p, positions, norm_g, w_in, ln_v_g, ln_v_b, sgu_w, sgu_b, w_a_out, pool_w, pool_scale,
           w_b_out, cq_norm_g, w_uq, ckv_norm_g, w_ukv, q_norm_g, k_norm_g, w_c_out, w_d_out, w_o,
           w_ple, ple_norm_g, w_ple_gate):
    batch, seq, d = x.shape
    depth = w_in.shape[0]
    t = batch * seq
    x2d = x.reshape(t, d)
    p2d = p.reshape(depth * t, PLE_DIM)
    assert d == D_MODEL and seq % 1024 == 0 and t % 2048 == 0, (batch, seq, d)
    cos_l, sin_l = _rope_tables(positions)

    row = lambda g: g.reshape(depth, 1, -1)
    bf16 = lambda w: w.astype(BF16)
    w_in_p = _stage_w_in(w_in)
    w_uq_p = _take_columns(bf16(w_uq), _uq_segments())
    qk_gain = lambda g: jnp.concatenate([g, g[:, MLA_NOPE:]], axis=1).reshape(depth, 1, MLA_QK_PAD)

    for i in range(depth):
        proj = _inproj(x2d, row(norm_g), w_in_p, i)
        q, k, v = _mla_prep(proj, cos_l, sin_l, i, row(cq_norm_g), w_uq_p, row(ckv_norm_g), bf16(w_ukv),
                            qk_gain(q_norm_g), qk_gain(k_norm_g))
        oc = _mla_attn(q, k, v, batch, seq)
        od = _sb_attn(proj, batch, seq)
        x2d = _mix_merge(oc, od, proj, x2d, p2d, seq, i, row(ln_v_g), row(ln_v_b), sgu_w,
                         jnp.swapaxes(sgu_b, 1, 2), bf16(pool_w), row(pool_scale), bf16(w_a_out),
                         bf16(w_b_out), bf16(w_c_out), bf16(w_d_out), bf16(w_o), bf16(w_ple),
                         row(ple_norm_g), bf16(w_ple_gate))
    return x2d.reshape(batch, seq, d)
```

```python
import functools
import math

import numpy as np
import jax
import jax.numpy as jnp
from jax import lax
from jax.experimental import pallas as pl
from jax.experimental.pallas import tpu as pltpu

F32 = jnp.float32
BF16 = jnp.bfloat16

EPS = 1e-6
NEG_INF = -1e30
CHUNK = 64
D_MODEL = 1024
PLE_DIM = 256
SGU_BLOCK = 128
GROUP_DIM = 128
N_GROUPS = 4
BRANCH_WIDTH = 512
POOL_WINDOWS = (2, 4, 8, 16)
POOL_HALO = 16
HEADS = 4
MLA_NOPE = 128
MLA_ROPE = 64
MLA_QK = MLA_NOPE + MLA_ROPE
MLA_Q_RANK = 256
MLA_KV_RANK = 128
MLA_QK_PAD = 256
HEAD_DIM = 128
ROPE_THETA = 10000.0

LANES = 128
VMEM_LIMIT = 56 * 1024 * 1024

OFF_GATES = 0
OFF_AU = 4096
OFF_AV = OFF_AU + 512
OFF_AZ = OFF_AV + 512
OFF_BX = OFF_AZ + 512
OFF_BZ = OFF_BX + 512
OFF_CZ = OFF_BZ + 512
OFF_DZ = OFF_CZ + 512
OFF_DQ = OFF_DZ + 512
OFF_DK = OFF_DQ + 512
OFF_DV = OFF_DK + 512
OFF_CQ = OFF_DV + 512
OFF_CKV = OFF_CQ + MLA_Q_RANK
OFF_CKR = OFF_CKV + MLA_KV_RANK
N_PROJ = OFF_CKR + 2 * MLA_ROPE

SB_UNDERFLOW_LOG2 = 152.0


def _reference_layout():
    sizes = (512, 512, 512, 512, 512, MLA_Q_RANK, MLA_KV_RANK, MLA_ROPE, 512, 512, 512, 512, 512, 4096)
    names = ("au", "av", "az", "bx", "bz", "cq", "ckv", "ckr", "cz", "dq", "dk", "dv", "dz", "gates")
    starts = np.cumsum((0,) + sizes[:-1])
    return {n: (int(s), int(z)) for n, s, z in zip(names, starts, sizes)}


def _proj_segments():
    layout = _reference_layout()
    order = ("gates", "au", "av", "az", "bx", "bz", "cz", "dz", "dq", "dk", "dv", "cq", "ckv", "ckr", "ckr")
    return [layout[n] for n in order]


def _uq_segments():
    nope = [(h * MLA_QK, MLA_NOPE) for h in range(HEADS)]
    rope = [(h * MLA_QK + MLA_NOPE, MLA_ROPE) for h in range(HEADS)]
    return nope + rope


def _take_columns(w, segments):
    return jnp.concatenate([w[..., s:s + n] for s, n in segments], axis=-1)


def _sigmoid(y):
    return 0.5 * jnp.tanh(0.5 * y) + 0.5


def _silu(z):
    return z * _sigmoid(z)


def _silu_gated(o, z):
    h = 0.5 * z
    return o * (h + h * jnp.tanh(h))


def _lane_tiles(x, n):
    return jnp.concatenate([x] * n, axis=1)


def _params(semantics):
    return pltpu.CompilerParams(dimension_semantics=semantics, vmem_limit_bytes=VMEM_LIMIT)


def _chunks(total, width):
    out, start = [], 0
    while start < total:
        size = min(width, total - start)
        out.append((start, size))
        start += size
    return out


def _stage_w_in_kernel(wt_ref, o_ref):
    dq_start, _ = _reference_layout()["dq"]
    ckr = _reference_layout()["ckr"]
    off = 0
    for start, size in _proj_segments():
        if (start, size) == ckr:
            if off != OFF_CKR:
                continue
            piece = jnp.concatenate([wt_ref[start:start + size, :]] * 2, axis=0)
        else:
            piece = wt_ref[start:start + size, :]
        if start == dq_start:
            piece = piece * (HEAD_DIM ** -0.5 * math.log2(math.e))
        rows = piece.shape[0]
        o_ref[:, off:off + rows] = piece.T.astype(BF16)
        off += rows


def _stage_w_in(w_in):
    depth, d, n_in = w_in.shape
    td = 128
    return pl.pallas_call(
        _stage_w_in_kernel,
        grid=(depth, d // td),
        in_specs=[pl.BlockSpec((None, n_in, td), lambda l, r: (l, 0, r))],
        out_specs=pl.BlockSpec((None, td, N_PROJ), lambda l, r: (l, r, 0)),
        out_shape=jax.ShapeDtypeStruct((depth, d, N_PROJ), BF16),
        compiler_params=_params(("parallel", "parallel")),
        name="stage_w_in",
    )(jnp.swapaxes(w_in, 1, 2))


def _inproj_kernel(x_ref, g_ref, w_ref, o_ref):
    x = x_ref[...]
    ms = jnp.mean(x * x, axis=-1, keepdims=True)
    h = (x * lax.rsqrt(ms + EPS) * g_ref[...]).astype(BF16)
    for start, size in _chunks(o_ref.shape[1], 512):
        cols = slice(start, start + size)
        o_ref[:, cols] = jnp.dot(h, w_ref[:, cols], preferred_element_type=F32).astype(BF16)


def _layer_spec(stacked, layer, single_buffer=False):
    shape = stacked.shape[1:]
    mode = pl.Buffered(1) if single_buffer else None
    return pl.BlockSpec((None,) + shape, lambda *_: (layer,) + (0,) * len(shape), pipeline_mode=mode)


def _inproj(x2d, g, w, layer):
    t, d = x2d.shape
    tm, tn = 1024, N_PROJ // 2
    return pl.pallas_call(
        _inproj_kernel,
        grid=(t // tm, N_PROJ // tn),
        in_specs=[
            pl.BlockSpec((tm, d), lambda i, j: (i, 0)),
            _layer_spec(g, layer),
            pl.BlockSpec((None, d, tn), lambda i, j: (layer, 0, j)),
        ],
        out_specs=pl.BlockSpec((tm, tn), lambda i, j: (i, j)),
        out_shape=jax.ShapeDtypeStruct((t, N_PROJ), BF16),
        compiler_params=_params(("parallel", "arbitrary")),
        name="inproj",
    )(x2d, g, w)


def _rope_table_kernel(pos_ref, freq_ref, sign_ref, cos_ref, sin_ref):
    ang = freq_ref[...] * pos_ref[...].astype(F32)
    reps = LANES // ang.shape[0]
    cos_ref[...] = jnp.concatenate([jnp.cos(ang)] * reps, axis=0).T
    sin_ref[...] = (jnp.concatenate([jnp.sin(ang)] * reps, axis=0) * sign_ref[...]).T


def _rope_tables(positions):
    t = positions.size
    half = MLA_ROPE // 2
    tt = 2048
    freqs = (ROPE_THETA ** (-jnp.arange(half, dtype=F32) / half)).reshape(half, 1)
    sign = np.where((np.arange(LANES) % MLA_ROPE) < half, -1.0, 1.0).astype(np.float32).reshape(LANES, 1)
    return pl.pallas_call(
        _rope_table_kernel,
        grid=(t // tt,),
        in_specs=[pl.BlockSpec((1, tt), lambda i: (0, i)),
                  pl.BlockSpec((half, 1), lambda i: (0, 0)),
                  pl.BlockSpec((LANES, 1), lambda i: (0, 0))],
        out_specs=[pl.BlockSpec((tt, LANES), lambda i: (i, 0)),
                   pl.BlockSpec((tt, LANES), lambda i: (i, 0))],
        out_shape=[jax.ShapeDtypeStruct((t, LANES), F32)] * 2,
        compiler_params=_params(("parallel",)),
        name="rope_tables",
    )(positions.reshape(1, t), freqs, sign)


def _mla_prep_kernel(cq_ref, ckv_ref, ckr_ref, cos_ref, sin_ref, cqg_ref, wuq_ref, ckvg_ref,
                     wukv_ref, qng_ref, kng_ref, q_ref, k_ref, v_ref):
    cos = cos_ref[...]
    sin = sin_ref[...]
    lane = lax.broadcasted_iota(jnp.int32, (1, LANES), 1)
    first_half = (lane % MLA_ROPE) < (MLA_ROPE // 2)
    low_group = lane < MLA_ROPE
    inv_qk = 1.0 / MLA_QK
    scale = MLA_QK ** -0.5 * math.log2(math.e)

    def rotary(xr):
        partner = jnp.where(first_half, pltpu.roll(xr, LANES - MLA_ROPE // 2, 1),
                            pltpu.roll(xr, MLA_ROPE // 2, 1))
        return xr * cos + partner * sin

    def sum_squares(parts):
        sq = jnp.concatenate([(x * x).astype(BF16) for x, _ in parts], axis=1)
        wts = jnp.concatenate([w for _, w in parts], axis=0)
        return jnp.dot(sq, wts, preferred_element_type=F32)

    def rms(xf, g):
        ones = jnp.ones((xf.shape[1], LANES), BF16)
        r = lax.rsqrt(sum_squares([(xf, ones)]) * (1.0 / xf.shape[1]) + EPS)
        return xf * _lane_tiles(r, xf.shape[1] // LANES) * g

    ones_tile = jnp.ones((LANES, LANES), BF16)
    half_tile = jnp.full((LANES, LANES), 0.5, BF16)
    group_row = lax.broadcasted_iota(jnp.int32, (LANES, LANES), 0) < MLA_ROPE
    group_tile = (jnp.where(group_row, 1.0, 0.0).astype(BF16), jnp.where(group_row, 0.0, 1.0).astype(BF16))

    qng = qng_ref[...]
    kng = kng_ref[...]
    q_raw = jnp.dot(rms(cq_ref[...].astype(F32), cqg_ref[...]).astype(BF16), wuq_ref[...],
                    preferred_element_type=F32)
    kv_raw = jnp.dot(rms(ckv_ref[...].astype(F32), ckvg_ref[...]).astype(BF16), wukv_ref[...],
                     preferred_element_type=F32)
    ckr = ckr_ref[...].astype(F32)
    ts = ckr.shape[0]

    kr_rot = rotary(ckr * kng[:, MLA_NOPE:])
    for h in range(HEADS):
        kn = kv_raw[:, 2 * h * HEAD_DIM:(2 * h + 1) * HEAD_DIM]
        r = lax.rsqrt(sum_squares([(kn, ones_tile), (ckr, half_tile)]) * inv_qk + EPS)
        mine = low_group if h % 2 == 0 else jnp.logical_not(low_group)
        k_ref[h, :, 0:MLA_NOPE] = (kn * r * kng[:, :MLA_NOPE]).astype(BF16)
        k_ref[h, :, MLA_NOPE:] = jnp.where(mine, kr_rot * r, 0.0).astype(BF16)
        v_ref[h, :, 0:HEAD_DIM] = kv_raw[:, (2 * h + 1) * HEAD_DIM:(2 * h + 2) * HEAD_DIM].astype(BF16)
        v_ref[h, :, HEAD_DIM:] = jnp.ones((ts, LANES), BF16)

    for j in range(HEADS // 2):
        blk = q_raw[:, HEADS * MLA_NOPE + j * LANES:HEADS * MLA_NOPE + (j + 1) * LANES]
        rs = []
        for h in (2 * j, 2 * j + 1):
            qn = q_raw[:, h * MLA_NOPE:(h + 1) * MLA_NOPE]
            r = lax.rsqrt(sum_squares([(qn, ones_tile), (blk, group_tile[h % 2])]) * inv_qk + EPS)
            rs.append(r)
            q_ref[h, :, 0:MLA_NOPE] = (qn * r * (qng[:, :MLA_NOPE] * scale)).astype(BF16)
        rot = rotary(blk * jnp.where(low_group, rs[0], rs[1]) * (qng[:, MLA_NOPE:] * scale))
        q_ref[2 * j, :, MLA_NOPE:] = jnp.where(low_group, rot, 0.0).astype(BF16)
        q_ref[2 * j + 1, :, MLA_NOPE:] = jnp.where(low_group, 0.0, rot).astype(BF16)


def _mla_prep(proj, cos_l, sin_l, layer, cqg, wuq, ckvg, wukv, qng, kng):
    t = proj.shape[0]
    ts = 1024
    head_major = pl.BlockSpec((HEADS, ts, MLA_QK_PAD), lambda i: (0, i, 0))
    return pl.pallas_call(
        _mla_prep_kernel,
        grid=(t // ts,),
        in_specs=[
            pl.BlockSpec((ts, MLA_Q_RANK), lambda i: (i, OFF_CQ // MLA_Q_RANK)),
            pl.BlockSpec((ts, MLA_KV_RANK), lambda i: (i, OFF_CKV // MLA_KV_RANK)),
            pl.BlockSpec((ts, LANES), lambda i: (i, OFF_CKR // LANES)),
            pl.BlockSpec((ts, LANES), lambda i: (i, 0)),
            pl.BlockSpec((ts, LANES), lambda i: (i, 0)),
        ] + [_layer_spec(a, layer) for a in (cqg, wuq, ckvg, wukv, qng, kng)],
        out_specs=[head_major] * 3,
        out_shape=[jax.ShapeDtypeStruct((HEADS, t, MLA_QK_PAD), BF16)] * 3,
        compiler_params=_params(("parallel",)),
        name="mla_prep",
    )(proj, proj, proj, cos_l, sin_l, cqg, wuq, ckvg, wukv, qng, kng)


def _mla_attn_kernel(q_ref, k_ref, v_ref, o_ref, m_sc, acc_sc, sa_sc, sb_sc, *, tq):
    i = pl.program_id(2)
    tk = tq // 2
    everything = slice(0, tq)
    upper, lower = slice(0, tk), slice(tk, tq)

    def block_rows(b):
        return pl.ds(pl.multiple_of(b * tk, tk), tk)

    def scores(b, s_ref, rows=everything):
        s_ref[rows, :] = lax.dot_general(q_ref[rows, :], k_ref[block_rows(b), :],
                                         (((1,), (1,)), ((), ())), preferred_element_type=F32)

    def mask_diagonal(s_ref, rows):
        for c in range(tk // CHUNK - 1):
            hidden = tk - (c + 1) * CHUNK
            s_ref[rows.start + c * CHUNK:rows.start + (c + 1) * CHUNK, tk - hidden:] = jnp.full(
                (CHUNK, hidden), NEG_INF, F32)

    def consume(b, s_ref, rows=everything):
        m_prev = m_sc[rows, :]
        m_new = jnp.maximum(m_prev, jnp.max(s_ref[rows, :], axis=-1, keepdims=True))
        p = jnp.exp2(s_ref[rows, :] - _lane_tiles(m_new, tk // LANES)).astype(BF16)
        alpha = _lane_tiles(jnp.exp2(m_prev - m_new), MLA_QK_PAD // LANES)
        acc_sc[rows, :] = (alpha * acc_sc[rows, :]
                           + jnp.dot(p, v_ref[block_rows(b), :], preferred_element_type=F32))
        m_sc[rows, :] = m_new

    m_sc[...] = jnp.full_like(m_sc, -jnp.inf)
    acc_sc[...] = jnp.zeros_like(acc_sc)
    scores(0, sa_sc, upper)
    scores(0, sa_sc, lower)

    def pairs(first_block, n_pairs):
        for n in range(n_pairs):
            b = first_block + 2 * n
            scores(b + 1, sb_sc)
            consume(b, sa_sc)
            scores(b + 2, sa_sc)
            consume(b + 1, sb_sc)

    def quad(jj, carry):
        pairs(4 * jj, 2)
        return carry

    lax.fori_loop(0, i // 2, quad, 0)

    @pl.when(i % 2 == 1)
    def _():
        pairs(2 * (i - 1), 1)

    scores(2 * i + 1, sb_sc, lower)
    mask_diagonal(sa_sc, upper)
    consume(2 * i, sa_sc)
    mask_diagonal(sb_sc, lower)
    consume(2 * i + 1, sb_sc, lower)

    acc = acc_sc[...]
    o_ref[...] = (acc[:, :HEAD_DIM] / acc[:, HEAD_DIM:]).astype(BF16)


def _mla_attn(q, k, v, batch, seq):
    tq = 1024
    nq = seq // tq
    return pl.pallas_call(
        functools.partial(_mla_attn_kernel, tq=tq),
        grid=(batch, HEADS, nq),
        in_specs=[
            pl.BlockSpec((None, tq, MLA_QK_PAD), lambda b, h, i: (h, b * nq + i, 0)),
            pl.BlockSpec((None, seq, MLA_QK_PAD), lambda b, h, i: (h, b, 0)),
            pl.BlockSpec((None, seq, MLA_QK_PAD), lambda b, h, i: (h, b, 0)),
        ],
        out_specs=pl.BlockSpec((tq, HEAD_DIM), lambda b, h, i: (b * nq + i, h)),
        out_shape=jax.ShapeDtypeStruct((batch * seq, BRANCH_WIDTH), BF16),
        scratch_shapes=[pltpu.VMEM((tq, LANES), F32), pltpu.VMEM((tq, MLA_QK_PAD), F32),
                        pltpu.VMEM((tq, tq // 2), F32), pltpu.VMEM((tq, tq // 2), F32)],
        compiler_params=_params(("parallel", "parallel", "arbitrary")),
        name="mla_attn",
    )(q, k, v)


def _sb_attn_kernel(q_ref, k_ref, v_ref, o_ref, acc_sc, run_sc, low_sm, *, tq):
    i = pl.program_id(1)
    row = lax.broadcasted_iota(jnp.int32, (tq, tq), 0)
    col = lax.broadcasted_iota(jnp.int32, (tq, tq), 1)
    tri = jnp.where(row >= col, 1.0, 0.0).astype(BF16)
    strict = col < row
    sign_bit = jnp.uint32(0x80000000)

    def tiles(specs):
        chains = [(pl.ds(pl.multiple_of(j * tq, tq), tq), masked, slice(h * HEAD_DIM, (h + 1) * HEAD_DIM), h)
                  for j, masked in specs for h in range(HEADS)]
        z = [lax.dot_general(q_ref[:, cols], k_ref[rows, cols], (((1,), (1,)), ((), ())),
                             preferred_element_type=F32) for rows, _, cols, _ in chains]
        pen16, total = [], []
        for n, (_, masked, _, _) in enumerate(chains):
            neg_abs = lax.bitcast_convert_type(lax.bitcast_convert_type(z[n], jnp.uint32) | sign_bit, F32)
            pen = jnp.maximum(z[n], 0.0) + jnp.log2(1.0 + jnp.exp2(neg_abs))
            if masked:
                pen = jnp.where(strict, pen, 0.0)
            pen16.append(pen.astype(BF16))
            total.append(jnp.sum(pen, axis=-1, keepdims=True))
        c = [jnp.dot(s, tri, preferred_element_type=F32) for s in pen16]
        run = [run_sc[h] for h in range(HEADS)]
        a = []
        for n, (_, masked, _, h) in enumerate(chains):
            an = jnp.exp2(z[n] - c[n] - _lane_tiles(run[h], tq // LANES))
            if masked:
                an = jnp.where(strict, an, 0.0)
            a.append(an.astype(BF16))
            run[h] = run[h] + total[n]
        pv = [jnp.dot(a[n], v_ref[rows, cols], preferred_element_type=F32)
              for n, (rows, _, cols, _) in enumerate(chains)]
        for n, (_, _, cols, _) in enumerate(chains):
            acc_sc[:, cols] += pv[n]
        low = None
        for h in range(HEADS):
            run_sc[h] = run[h]
            low = run[h] if low is None else jnp.minimum(low, run[h])
        return jnp.min(low)

    acc_sc[...] = jnp.zeros_like(acc_sc)
    run_sc[...] = jnp.zeros_like(run_sc)

    @pl.when(i == 0)
    def _():
        low_sm[0] = tiles([(i, True)])

    @pl.when(i > 0)
    def _():
        low_sm[0] = tiles([(i, True), (i - 1, False)])

    def cond(state):
        j, low = state
        return jnp.logical_and(j >= 0, low < SB_UNDERFLOW_LOG2)

    def body(state):
        j, _ = state
        return j - 1, tiles([(j, False)])

    lax.while_loop(cond, body, (i - 2, low_sm[0]))
    o_ref[...] = acc_sc[...].astype(BF16)


def _sb_attn(proj, batch, seq):
    tq = 256
    nq = seq // tq
    w = BRANCH_WIDTH
    resident = lambda off: pl.BlockSpec((seq, w), lambda b, i: (b, off // w),
                                        pipeline_mode=pl.Buffered(1))
    return pl.pallas_call(
        functools.partial(_sb_attn_kernel, tq=tq),
        grid=(batch, nq),
        in_specs=[
            pl.BlockSpec((tq, w), lambda b, i: (b * nq + i, OFF_DQ // w)),
            resident(OFF_DK),
            resident(OFF_DV),
        ],
        out_specs=pl.BlockSpec((tq, w), lambda b, i: (b * nq + i, 0)),
        out_shape=jax.ShapeDtypeStruct((batch * seq, w), BF16),
        scratch_shapes=[pltpu.VMEM((tq, w), F32), pltpu.VMEM((HEADS, tq, LANES), F32),
                        pltpu.SMEM((1,), F32)],
        compiler_params=_params(("parallel", "arbitrary")),
        name="sb_attn",
    )(proj, proj, proj)


def _local_kernel(au_ref, av_ref, az_ref, bx_ref, halo_ref, bz_ref, lng_ref, lnb_ref, sw_ref, sb_ref,
                  pw_ref, ps_ref, ya_ref, yb_ref, *, tm, seq):
    i = pl.program_id(0)

    v = av_ref[...].astype(F32)
    mu = jnp.mean(v, axis=-1, keepdims=True)
    vc = v - mu
    vn = (vc * lax.rsqrt(jnp.mean(vc * vc, axis=-1, keepdims=True) + EPS) * lng_ref[...]
          + lnb_ref[...]).astype(BF16)
    tpos = lax.broadcasted_iota(jnp.int32, (SGU_BLOCK, SGU_BLOCK), 0) // CHUNK
    spos = lax.broadcasted_iota(jnp.int32, (SGU_BLOCK, SGU_BLOCK), 1) // CHUNK
    bias_t = sb_ref[...]
    for g in range(N_GROUPS):
        w = jnp.where(spos <= tpos, sw_ref[g], 0.0).astype(BF16)
        cols = slice(g * GROUP_DIM, (g + 1) * GROUP_DIM)
        for n in range(tm // SGU_BLOCK):
            rows = slice(n * SGU_BLOCK, (n + 1) * SGU_BLOCK)
            mixed = jnp.dot(w, vn[rows, cols], preferred_element_type=F32) + bias_t[:, g:g + 1]
            ya_ref[rows, cols] = (au_ref[rows, cols].astype(F32) * mixed
                                  * _silu(az_ref[rows, cols].astype(F32))).astype(BF16)

    start = (i * tm) % seq
    halo = jnp.where(start == 0, 0.0, halo_ref[...].astype(F32))
    xfull = jnp.concatenate([halo, bx_ref[...].astype(F32)], axis=0)
    pos = start + lax.broadcasted_iota(jnp.int32, (tm, 1), 0)
    for g, win in enumerate(POOL_WINDOWS):
        cols = slice(g * GROUP_DIM, (g + 1) * GROUP_DIM)
        xg = xfull[:, cols]
        s, shift = xg, 1
        while shift < win:
            s = s + pltpu.roll(s, shift, 0)
            shift *= 2
        count = jnp.minimum(pos + 1, win).astype(F32)
        pooled = s[POOL_HALO:] / count - xg[POOL_HALO:]
        y = jnp.dot(pooled.astype(BF16), pw_ref[g], preferred_element_type=F32) * ps_ref[:, cols]
        yb_ref[:, cols] = (y * _silu(bz_ref[:, cols].astype(F32))).astype(BF16)


def _mix_merge_kernel(au_ref, av_ref, az_ref, bx_ref, halo_ref, bz_ref, oc_ref, od_ref, cz_ref, dz_ref,
                      gates_ref, x_ref, p_ref, lng_ref, lnb_ref, sw_ref, sb_ref, pw_ref, ps_ref,
                      wa_ref, wb_ref, wc_ref, wd_ref, wo_ref, wple_ref, pleg_ref, wpg_ref, o_ref,
                      ya_sc, yb_sc, *, tm, seq):
    def gated(n, y, w_ref):
        gate = _sigmoid(gates_ref[:, n * D_MODEL:(n + 1) * D_MODEL])
        return gate.astype(F32) * jnp.dot(y, w_ref[...], preferred_element_type=F32)

    yc = _silu_gated(oc_ref[...], cz_ref[...])
    yd = _silu_gated(od_ref[...], dz_ref[...])
    merged = gated(2, yc, wc_ref) + gated(3, yd, wd_ref)
    e = jnp.dot(p_ref[...].astype(BF16), wple_ref[...], preferred_element_type=F32)

    _local_kernel(au_ref, av_ref, az_ref, bx_ref, halo_ref, bz_ref, lng_ref, lnb_ref, sw_ref, sb_ref,
                  pw_ref, ps_ref, ya_sc, yb_sc, tm=tm, seq=seq)

    merged = merged + gated(0, ya_sc[...], wa_ref) + gated(1, yb_sc[...], wb_ref)
    x1 = x_ref[...] + jnp.dot(merged.astype(BF16), wo_ref[...], preferred_element_type=F32)
    n1 = (x1 * lax.rsqrt(jnp.mean(x1 * x1, axis=-1, keepdims=True) + EPS) * pleg_ref[...]).astype(BF16)
    o_ref[...] = x1 + _sigmoid(jnp.dot(n1, wpg_ref[...], preferred_element_type=F32)) * e


def _mix_merge(oc, od, proj, x2d, p2d, seq, layer, lng, lnb, sgu_w, sgu_bt, pool_w, pool_scale,
               wa, wb, wc, wd, wo, wple, pleg, wpg):
    t, d = x2d.shape
    tm = 512
    w = BRANCH_WIDTH
    row = lambda width: pl.BlockSpec((tm, width), lambda i: (i, 0))
    col = lambda off: pl.BlockSpec((tm, w), lambda i: (i, off // w))
    halo = pl.BlockSpec((POOL_HALO, w),
                        lambda i: (jnp.maximum(i * (tm // POOL_HALO) - 1, 0), OFF_BX // w))
    params = (lng, lnb, sgu_w, sgu_bt, pool_w, pool_scale, wa, wb, wc, wd, wo, wple, pleg, wpg)
    return pl.pallas_call(
        functools.partial(_mix_merge_kernel, tm=tm, seq=seq),
        grid=(t // tm,),
        in_specs=[col(OFF_AU), col(OFF_AV), col(OFF_AZ), col(OFF_BX), halo, col(OFF_BZ),
                  row(w), row(w), col(OFF_CZ), col(OFF_DZ),
                  pl.BlockSpec((tm, 4 * d), lambda i: (i, OFF_GATES // (4 * d))),
                  row(d),
                  pl.BlockSpec((tm, PLE_DIM), lambda i: (layer * (t // tm) + i, 0))]
        + [_layer_spec(a, layer, single_buffer=True) for a in params],
        out_specs=row(d),
        out_shape=jax.ShapeDtypeStruct((t, d), F32),
        scratch_shapes=[pltpu.VMEM((tm, w), BF16), pltpu.VMEM((tm, w), BF16)],
        compiler_params=_params(("parallel",)),
        name="mix_merge",
    )(proj, proj, proj, proj, proj, proj, oc, od, proj, proj, proj, x2d, p2d, *params)


def kernel(x, p, positions, norm_g, w_in, ln_v_g, ln_v_b, sgu_w, sgu_b, w_a_out, pool_w, pool_scale,
           w_b_out, cq_norm_g, w_uq, ckv_norm_g, w_ukv, q_norm_g, k_norm_g, w_c_out, w_d_out, w_o,
           w_ple, ple_norm_g, w_ple_gate):
    batch, seq, d = x.shape
    depth = w_in.shape[0]
    t = batch * seq
    x2d = x.reshape(t, d)
    p2d = p.reshape(depth * t, PLE_DIM)
    assert d == D_MODEL and seq % 1024 == 0 and t % 2048 == 0, (batch, seq, d)
    cos_l, sin_l = _rope_tables(positions)

    row = lambda g: g.reshape(depth, 1, -1)
    bf16 = lambda w: w.astype(BF16)
    w_in_p = _stage_w_in(w_in)
    w_uq_p = _take_columns(bf16(w_uq), _uq_segments())
    qk_gain = lambda g: jnp.concatenate([g, g[:, MLA_NOPE:]], axis=1).reshape(depth, 1, MLA_QK_PAD)

    for i in range(depth):
        proj = _inproj(x2d, row(norm_g), w_in_p, i)
        q, k, v = _mla_prep(proj, cos_l, sin_l, i, row(cq_norm_g), w_uq_p, row(ckv_norm_g), bf16(w_ukv),
                            qk_gain(q_norm_g), qk_gain(k_norm_g))
        oc = _mla_attn(q, k, v, batch, seq)
        od = _sb_attn(proj, batch, seq)
        x2d = _mix_merge(oc, od, proj, x2d, p2d, seq, i, row(ln_v_g), row(ln_v_b), sgu_w,
                         jnp.swapaxes(sgu_b, 1, 2), bf16(pool_w), row(pool_scale), bf16(w_a_out),
                         bf16(w_b_out), bf16(w_c_out), bf16(w_d_out), bf16(w_o), bf16(w_ple),
                         row(ple_norm_g), bf16(w_ple_gate))
    return x2d.reshape(batch, seq, d)
```

```python
import functools
import math

import numpy as np
import jax
import jax.numpy as jnp
from jax import lax
from jax.experimental import pallas as pl
from jax.experimental.pallas import tpu as pltpu

F32 = jnp.float32
BF16 = jnp.bfloat16

EPS = 1e-6
NEG_INF = -1e30
CHUNK = 64
D_MODEL = 1024
PLE_DIM = 256
SGU_BLOCK = 128
GROUP_DIM = 128
N_GROUPS = 4
BRANCH_WIDTH = 512
POOL_WINDOWS = (2, 4, 8, 16)
POOL_HALO = 16
HEADS = 4
MLA_NOPE = 128
MLA_ROPE = 64
MLA_QK = MLA_NOPE + MLA_ROPE
MLA_Q_RANK = 256
MLA_KV_RANK = 128
MLA_QK_PAD = 256
HEAD_DIM = 128
ROPE_THETA = 10000.0

LANES = 128
VMEM_LIMIT = 56 * 1024 * 1024

OFF_GATES = 0
OFF_AU = 4096
OFF_AV = OFF_AU + 512
OFF_AZ = OFF_AV + 512
OFF_BX = OFF_AZ + 512
OFF_BZ = OFF_BX + 512
OFF_CZ = OFF_BZ + 512
OFF_DZ = OFF_CZ + 512
OFF_DQ = OFF_DZ + 512
OFF_DK = OFF_DQ + 512
OFF_DV = OFF_DK + 512
OFF_CQ = OFF_DV + 512
OFF_CKV = OFF_CQ + MLA_Q_RANK
OFF_CKR = OFF_CKV + MLA_KV_RANK
N_PROJ = OFF_CKR + 2 * MLA_ROPE

SB_UNDERFLOW_LOG2 = 152.0
QUERY_TILES_PER_STEP = 2


def _reference_layout():
    sizes = (512, 512, 512, 512, 512, MLA_Q_RANK, MLA_KV_RANK, MLA_ROPE, 512, 512, 512, 512, 512, 4096)
    names = ("au", "av", "az", "bx", "bz", "cq", "ckv", "ckr", "cz", "dq", "dk", "dv", "dz", "gates")
    starts = np.cumsum((0,) + sizes[:-1])
    return {n: (int(s), int(z)) for n, s, z in zip(names, starts, sizes)}


def _proj_segments():
    layout = _reference_layout()
    order = ("gates", "au", "av", "az", "bx", "bz", "cz", "dz", "dq", "dk", "dv", "cq", "ckv", "ckr", "ckr")
    return [layout[n] for n in order]


def _uq_segments():
    nope = [(h * MLA_QK, MLA_NOPE) for h in range(HEADS)]
    rope = [(h * MLA_QK + MLA_NOPE, MLA_ROPE) for h in range(HEADS)]
    return nope + rope


def _take_columns(w, segments):
    return jnp.concatenate([w[..., s:s + n] for s, n in segments], axis=-1)


def _sigmoid(y):
    return 0.5 * jnp.tanh(0.5 * y) + 0.5


def _silu(z):
    return z * _sigmoid(z)


def _silu_gated(o, z):
    h = 0.5 * z
    return o * (h + h * jnp.tanh(h))


def _lane_tiles(x, n):
    return jnp.concatenate([x] * n, axis=1)


def _params(semantics):
    return pltpu.CompilerParams(dimension_semantics=semantics, vmem_limit_bytes=VMEM_LIMIT)


def _chunks(total, width):
    out, start = [], 0
    while start < total:
        size = min(width, total - start)
        out.append((start, size))
        start += size
    return out


def _stage_w_in_kernel(wt_ref, o_ref):
    dq_start, _ = _reference_layout()["dq"]
    ckr = _reference_layout()["ckr"]
    off = 0
    for start, size in _proj_segments():
        if (start, size) == ckr:
            if off != OFF_CKR:
                continue
            piece = jnp.concatenate([wt_ref[start:start + size, :]] * 2, axis=0)
        else:
            piece = wt_ref[start:start + size, :]
        if start == dq_start:
            piece = piece * (HEAD_DIM ** -0.5 * math.log2(math.e))
        rows = piece.shape[0]
        o_ref[:, off:off + rows] = piece.T.astype(BF16)
        off += rows


def _stage_w_in(w_in):
    depth, d, n_in = w_in.shape
    td = 128
    return pl.pallas_call(
        _stage_w_in_kernel,
        grid=(depth, d // td),
        in_specs=[pl.BlockSpec((None, n_in, td), lambda l, r: (l, 0, r))],
        out_specs=pl.BlockSpec((None, td, N_PROJ), lambda l, r: (l, r, 0)),
        out_shape=jax.ShapeDtypeStruct((depth, d, N_PROJ), BF16),
        compiler_params=_params(("parallel", "parallel")),
        name="stage_w_in",
    )(jnp.swapaxes(w_in, 1, 2))


def _inproj_kernel(x_ref, g_ref, w_ref, o_ref):
    x = x_ref[...]
    ms = jnp.mean(x * x, axis=-1, keepdims=True)
    h = (x * lax.rsqrt(ms + EPS) * g_ref[...]).astype(BF16)
    for start, size in _chunks(o_ref.shape[1], 512):
        cols = slice(start, start + size)
        o_ref[:, cols] = jnp.dot(h, w_ref[:, cols], preferred_element_type=F32).astype(BF16)


def _layer_spec(stacked, layer, single_buffer=False):
    shape = stacked.shape[1:]
    mode = pl.Buffered(1) if single_buffer else None
    return pl.BlockSpec((None,) + shape, lambda *_: (layer,) + (0,) * len(shape), pipeline_mode=mode)


def _inproj(x2d, g, w, layer):
    t, d = x2d.shape
    tm, tn = 1024, N_PROJ // 2
    return pl.pallas_call(
        _inproj_kernel,
        grid=(t // tm, N_PROJ // tn),
        in_specs=[
            pl.BlockSpec((tm, d), lambda i, j: (i, 0)),
            _layer_spec(g, layer),
            pl.BlockSpec((None, d, tn), lambda i, j: (layer, 0, j)),
        ],
        out_specs=pl.BlockSpec((tm, tn), lambda i, j: (i, j)),
        out_shape=jax.ShapeDtypeStruct((t, N_PROJ), BF16),
        compiler_params=_params(("parallel", "arbitrary")),
        name="inproj",
    )(x2d, g, w)


def _rope_table_kernel(pos_ref, freq_ref, sign_ref, cos_ref, sin_ref):
    ang = freq_ref[...] * pos_ref[...].astype(F32)
    reps = LANES // ang.shape[0]
    cos_ref[...] = jnp.concatenate([jnp.cos(ang)] * reps, axis=0).T
    sin_ref[...] = (jnp.concatenate([jnp.sin(ang)] * reps, axis=0) * sign_ref[...]).T


def _rope_tables(positions):
    t = positions.size
    half = MLA_ROPE // 2
    tt = 2048
    freqs = (ROPE_THETA ** (-jnp.arange(half, dtype=F32) / half)).reshape(half, 1)
    sign = np.where((np.arange(LANES) % MLA_ROPE) < half, -1.0, 1.0).astype(np.float32).reshape(LANES, 1)
    return pl.pallas_call(
        _rope_table_kernel,
        grid=(t // tt,),
        in_specs=[pl.BlockSpec((1, tt), lambda i: (0, i)),
                  pl.BlockSpec((half, 1), lambda i: (0, 0)),
                  pl.BlockSpec((LANES, 1), lambda i: (0, 0))],
        out_specs=[pl.BlockSpec((tt, LANES), lambda i: (i, 0)),
                   pl.BlockSpec((tt, LANES), lambda i: (i, 0))],
        out_shape=[jax.ShapeDtypeStruct((t, LANES), F32)] * 2,
        compiler_params=_params(("parallel",)),
        name="rope_tables",
    )(positions.reshape(1, t), freqs, sign)


def _mla_prep_kernel(cq_ref, ckv_ref, ckr_ref, cos_ref, sin_ref, cqg_ref, wuq_ref, ckvg_ref,
                     wukv_ref, qng_ref, kng_ref, q_ref, k_ref, v_ref):
    cos = cos_ref[...]
    sin = sin_ref[...]
    lane = lax.broadcasted_iota(jnp.int32, (1, LANES), 1)
    first_half = (lane % MLA_ROPE) < (MLA_ROPE // 2)
    low_group = lane < MLA_ROPE
    inv_qk = 1.0 / MLA_QK
    scale = MLA_QK ** -0.5 * math.log2(math.e)

    def rotary(xr):
        partner = jnp.where(first_half, pltpu.roll(xr, LANES - MLA_ROPE // 2, 1),
                            pltpu.roll(xr, MLA_ROPE // 2, 1))
        return xr * cos + partner * sin

    def sum_squares(parts):
        sq = jnp.concatenate([(x * x).astype(BF16) for x, _ in parts], axis=1)
        wts = jnp.concatenate([w for _, w in parts], axis=0)
        return jnp.dot(sq, wts, preferred_element_type=F32)

    def rms(xf, g):
        ones = jnp.ones((xf.shape[1], LANES), BF16)
        r = lax.rsqrt(sum_squares([(xf, ones)]) * (1.0 / xf.shape[1]) + EPS)
        return xf * _lane_tiles(r, xf.shape[1] // LANES) * g

    ones_tile = jnp.ones((LANES, LANES), BF16)
    half_tile = jnp.full((LANES, LANES), 0.5, BF16)
    group_row = lax.broadcasted_iota(jnp.int32, (LANES, LANES), 0) < MLA_ROPE
    group_tile = (jnp.where(group_row, 1.0, 0.0).astype(BF16), jnp.where(group_row, 0.0, 1.0).astype(BF16))

    qng = qng_ref[...]
    kng = kng_ref[...]
    q_raw = jnp.dot(rms(cq_ref[...].astype(F32), cqg_ref[...]).astype(BF16), wuq_ref[...],
                    preferred_element_type=F32)
    kv_raw = jnp.dot(rms(ckv_ref[...].astype(F32), ckvg_ref[...]).astype(BF16), wukv_ref[...],
                     preferred_element_type=F32)
    ckr = ckr_ref[...].astype(F32)
    ts = ckr.shape[0]

    kr_rot = rotary(ckr * kng[:, MLA_NOPE:])
    for h in range(HEADS):
        kn = kv_raw[:, 2 * h * HEAD_DIM:(2 * h + 1) * HEAD_DIM]
        r = lax.rsqrt(sum_squares([(kn, ones_tile), (ckr, half_tile)]) * inv_qk + EPS)
        mine = low_group if h % 2 == 0 else jnp.logical_not(low_group)
        k_ref[h, :, 0:MLA_NOPE] = (kn * r * kng[:, :MLA_NOPE]).astype(BF16)
        k_ref[h, :, MLA_NOPE:] = jnp.where(mine, kr_rot * r, 0.0).astype(BF16)
        v_ref[h, :, 0:HEAD_DIM] = kv_raw[:, (2 * h + 1) * HEAD_DIM:(2 * h + 2) * HEAD_DIM].astype(BF16)
        v_ref[h, :, HEAD_DIM:] = jnp.ones((ts, LANES), BF16)

    for j in range(HEADS // 2):
        blk = q_raw[:, HEADS * MLA_NOPE + j * LANES:HEADS * MLA_NOPE + (j + 1) * LANES]
        rs = []
        for h in (2 * j, 2 * j + 1):
            qn = q_raw[:, h * MLA_NOPE:(h + 1) * MLA_NOPE]
            r = lax.rsqrt(sum_squares([(qn, ones_tile), (blk, group_tile[h % 2])]) * inv_qk + EPS)
            rs.append(r)
            q_ref[h, :, 0:MLA_NOPE] = (qn * r * (qng[:, :MLA_NOPE] * scale)).astype(BF16)
        rot = rotary(blk * jnp.where(low_group, rs[0], rs[1]) * (qng[:, MLA_NOPE:] * scale))
        q_ref[2 * j, :, MLA_NOPE:] = jnp.where(low_group, rot, 0.0).astype(BF16)
        q_ref[2 * j + 1, :, MLA_NOPE:] = jnp.where(low_group, 0.0, rot).astype(BF16)


def _mla_prep(proj, cos_l, sin_l, layer, cqg, wuq, ckvg, wukv, qng, kng):
    t = proj.shape[0]
    ts = 1024
    head_major = pl.BlockSpec((HEADS, ts, MLA_QK_PAD), lambda i: (0, i, 0))
    return pl.pallas_call(
        _mla_prep_kernel,
        grid=(t // ts,),
        in_specs=[
            pl.BlockSpec((ts, MLA_Q_RANK), lambda i: (i, OFF_CQ // MLA_Q_RANK)),
            pl.BlockSpec((ts, MLA_KV_RANK), lambda i: (i, OFF_CKV // MLA_KV_RANK)),
            pl.BlockSpec((ts, LANES), lambda i: (i, OFF_CKR // LANES)),
            pl.BlockSpec((ts, LANES), lambda i: (i, 0)),
            pl.BlockSpec((ts, LANES), lambda i: (i, 0)),
        ] + [_layer_spec(a, layer) for a in (cqg, wuq, ckvg, wukv, qng, kng)],
        out_specs=[head_major] * 3,
        out_shape=[jax.ShapeDtypeStruct((HEADS, t, MLA_QK_PAD), BF16)] * 3,
        compiler_params=_params(("parallel",)),
        name="mla_prep",
    )(proj, proj, proj, cos_l, sin_l, cqg, wuq, ckvg, wukv, qng, kng)


def _mla_attn_kernel(q_ref, k_ref, v_ref, o_ref, m_sc, acc_sc, sa_sc, sb_sc, *, tq):
    i = pl.program_id(2)
    tk = tq // 2
    everything = slice(0, tq)
    upper, lower = slice(0, tk), slice(tk, tq)

    def block_rows(b):
        return pl.ds(pl.multiple_of(b * tk, tk), tk)

    def scores(b, s_ref, rows=everything):
        s_ref[rows, :] = lax.dot_general(q_ref[rows, :], k_ref[block_rows(b), :],
                                         (((1,), (1,)), ((), ())), preferred_element_type=F32)

    def mask_diagonal(s_ref, rows):
        for c in range(tk // CHUNK - 1):
            hidden = tk - (c + 1) * CHUNK
            s_ref[rows.start + c * CHUNK:rows.start + (c + 1) * CHUNK, tk - hidden:] = jnp.full(
                (CHUNK, hidden), NEG_INF, F32)

    def consume(b, s_ref, rows=everything):
        m_prev = m_sc[rows, :]
        m_new = jnp.maximum(m_prev, jnp.max(s_ref[rows, :], axis=-1, keepdims=True))
        p = jnp.exp2(s_ref[rows, :] - _lane_tiles(m_new, tk // LANES)).astype(BF16)
        alpha = _lane_tiles(jnp.exp2(m_prev - m_new), MLA_QK_PAD // LANES)
        acc_sc[rows, :] = (alpha * acc_sc[rows, :]
                           + jnp.dot(p, v_ref[block_rows(b), :], preferred_element_type=F32))
        m_sc[rows, :] = m_new

    m_sc[...] = jnp.full_like(m_sc, -jnp.inf)
    acc_sc[...] = jnp.zeros_like(acc_sc)
    scores(0, sa_sc, upper)
    scores(0, sa_sc, lower)

    def pairs(first_block, n_pairs):
        for n in range(n_pairs):
            b = first_block + 2 * n
            scores(b + 1, sb_sc)
            consume(b, sa_sc)
            scores(b + 2, sa_sc)
            consume(b + 1, sb_sc)

    def quad(jj, carry):
        pairs(4 * jj, 2)
        return carry

    lax.fori_loop(0, i // 2, quad, 0)

    @pl.when(i % 2 == 1)
    def _():
        pairs(2 * (i - 1), 1)

    scores(2 * i + 1, sb_sc, lower)
    mask_diagonal(sa_sc, upper)
    consume(2 * i, sa_sc)
    mask_diagonal(sb_sc, lower)
    consume(2 * i + 1, sb_sc, lower)

    acc = acc_sc[...]
    o_ref[...] = (acc[:, :HEAD_DIM] / acc[:, HEAD_DIM:]).astype(BF16)


def _mla_attn(q, k, v, batch, seq):
    tq = 1024
    nq = seq // tq
    return pl.pallas_call(
        functools.partial(_mla_attn_kernel, tq=tq),
        grid=(batch, HEADS, nq),
        in_specs=[
            pl.BlockSpec((None, tq, MLA_QK_PAD), lambda b, h, i: (h, b * nq + i, 0)),
            pl.BlockSpec((None, seq, MLA_QK_PAD), lambda b, h, i: (h, b, 0)),
            pl.BlockSpec((None, seq, MLA_QK_PAD), lambda b, h, i: (h, b, 0)),
        ],
        out_specs=pl.BlockSpec((tq, HEAD_DIM), lambda b, h, i: (b * nq + i, h)),
        out_shape=jax.ShapeDtypeStruct((batch * seq, BRANCH_WIDTH), BF16),
        scratch_shapes=[pltpu.VMEM((tq, LANES), F32), pltpu.VMEM((tq, MLA_QK_PAD), F32),
                        pltpu.VMEM((tq, tq // 2), F32), pltpu.VMEM((tq, tq // 2), F32)],
        compiler_params=_params(("parallel", "parallel", "arbitrary")),
        name="mla_attn",
    )(q, k, v)


def _sb_attn_kernel(q_ref, k_ref, v_ref, o_ref, acc_sc, run_sc, low_sm, *, tq):
    row = lax.broadcasted_iota(jnp.int32, (tq, tq), 0)
    col = lax.broadcasted_iota(jnp.int32, (tq, tq), 1)
    tri = jnp.where(row >= col, 1.0, 0.0).astype(BF16)
    strict = col < row
    sign_bit = jnp.uint32(0x80000000)

    def tiles(specs, qrows):
        chains = [(pl.ds(pl.multiple_of(j * tq, tq), tq), masked, slice(h * HEAD_DIM, (h + 1) * HEAD_DIM), h)
                  for j, masked in specs for h in range(HEADS)]
        z = [lax.dot_general(q_ref[qrows, cols], k_ref[rows, cols], (((1,), (1,)), ((), ())),
                             preferred_element_type=F32) for rows, _, cols, _ in chains]
        pen16, total = [], []
        for n, (_, masked, _, _) in enumerate(chains):
            neg_abs = lax.bitcast_convert_type(lax.bitcast_convert_type(z[n], jnp.uint32) | sign_bit, F32)
            pen = jnp.maximum(z[n], 0.0) + jnp.log2(1.0 + jnp.exp2(neg_abs))
            if masked:
                pen = jnp.where(strict, pen, 0.0)
            pen16.append(pen.astype(BF16))
            total.append(jnp.sum(pen, axis=-1, keepdims=True))
        c = [jnp.dot(s, tri, preferred_element_type=F32) for s in pen16]
        run = [run_sc[h] for h in range(HEADS)]
        a = []
        for n, (_, masked, _, h) in enumerate(chains):
            an = jnp.exp2(z[n] - c[n] - _lane_tiles(run[h], tq // LANES))
            if masked:
                an = jnp.where(strict, an, 0.0)
            a.append(an.astype(BF16))
            run[h] = run[h] + total[n]
        pv = [jnp.dot(a[n], v_ref[rows, cols], preferred_element_type=F32)
              for n, (rows, _, cols, _) in enumerate(chains)]
        for n, (_, _, cols, _) in enumerate(chains):
            acc_sc[:, cols] += pv[n]
        low = None
        for h in range(HEADS):
            run_sc[h] = run[h]
            low = run[h] if low is None else jnp.minimum(low, run[h])
        return jnp.min(low)

    def query_tile(i, qrows):
        acc_sc[...] = jnp.zeros_like(acc_sc)
        run_sc[...] = jnp.zeros_like(run_sc)

        @pl.when(i == 0)
        def _():
            low_sm[0] = tiles([(i, True)], qrows)

        @pl.when(i > 0)
        def _():
            low_sm[0] = tiles([(i, True), (i - 1, False)], qrows)

        def cond(state):
            j, low = state
            return jnp.logical_and(j >= 0, low < SB_UNDERFLOW_LOG2)

        def body(state):
            j, _ = state
            return j - 1, tiles([(j, False)], qrows)

        lax.while_loop(cond, body, (i - 2, low_sm[0]))
        o_ref[qrows, :] = acc_sc[...].astype(BF16)

    for t in range(QUERY_TILES_PER_STEP):
        query_tile(QUERY_TILES_PER_STEP * pl.program_id(1) + t, slice(t * tq, (t + 1) * tq))


def _sb_attn(proj, batch, seq):
    tq = 256
    rows = QUERY_TILES_PER_STEP * tq
    nq = seq // rows
    w = BRANCH_WIDTH
    resident = lambda off: pl.BlockSpec((seq, w), lambda b, i: (b, off // w),
                                        pipeline_mode=pl.Buffered(1))
    return pl.pallas_call(
        functools.partial(_sb_attn_kernel, tq=tq),
        grid=(batch, nq),
        in_specs=[
            pl.BlockSpec((rows, w), lambda b, i: (b * nq + i, OFF_DQ // w)),
            resident(OFF_DK),
            resident(OFF_DV),
        ],
        out_specs=pl.BlockSpec((rows, w), lambda b, i: (b * nq + i, 0)),
        out_shape=jax.ShapeDtypeStruct((batch * seq, w), BF16),
        scratch_shapes=[pltpu.VMEM((tq, w), F32), pltpu.VMEM((HEADS, tq, LANES), F32),
                        pltpu.SMEM((1,), F32)],
        compiler_params=_params(("parallel", "arbitrary")),
        name="sb_attn",
    )(proj, proj, proj)


def _local_kernel(au_ref, av_ref, az_ref, bx_ref, halo_ref, bz_ref, lng_ref, lnb_ref, sw_ref, sb_ref,
                  pw_ref, ps_ref, ya_ref, yb_ref, *, tm, seq):
    i = pl.program_id(0)

    v = av_ref[...].astype(F32)
    mu = jnp.mean(v, axis=-1, keepdims=True)
    vc = v - mu
    vn = (vc * lax.rsqrt(jnp.mean(vc * vc, axis=-1, keepdims=True) + EPS) * lng_ref[...]
          + lnb_ref[...]).astype(BF16)
    tpos = lax.broadcasted_iota(jnp.int32, (SGU_BLOCK, SGU_BLOCK), 0) // CHUNK
    spos = lax.broadcasted_iota(jnp.int32, (SGU_BLOCK, SGU_BLOCK), 1) // CHUNK
    bias_t = sb_ref[...]
    for g in range(N_GROUPS):
        w = jnp.where(spos <= tpos, sw_ref[g], 0.0).astype(BF16)
        cols = slice(g * GROUP_DIM, (g + 1) * GROUP_DIM)
        for n in range(tm // SGU_BLOCK):
            rows = slice(n * SGU_BLOCK, (n + 1) * SGU_BLOCK)
            mixed = jnp.dot(w, vn[rows, cols], preferred_element_type=F32) + bias_t[:, g:g + 1]
            ya_ref[rows, cols] = (au_ref[rows, cols].astype(F32) * mixed
                                  * _silu(az_ref[rows, cols].astype(F32))).astype(BF16)

    start = (i * tm) % seq
    halo = jnp.where(start == 0, 0.0, halo_ref[...].astype(F32))
    xfull = jnp.concatenate([halo, bx_ref[...].astype(F32)], axis=0)
    pos = start + lax.broadcasted_iota(jnp.int32, (tm, 1), 0)
    for g, win in enumerate(POOL_WINDOWS):
        cols = slice(g * GROUP_DIM, (g + 1) * GROUP_DIM)
        xg = xfull[:, cols]
        s, shift = xg, 1
        while shift < win:
            s = s + pltpu.roll(s, shift, 0)
            shift *= 2
        count = jnp.minimum(pos + 1, win).astype(F32)
        pooled = s[POOL_HALO:] / count - xg[POOL_HALO:]
        y = jnp.dot(pooled.astype(BF16), pw_ref[g], preferred_element_type=F32) * ps_ref[:, cols]
        yb_ref[:, cols] = (y * _silu(bz_ref[:, cols].astype(F32))).astype(BF16)


def _mix_merge_kernel(au_ref, av_ref, az_ref, bx_ref, halo_ref, bz_ref, oc_ref, od_ref, cz_ref, dz_ref,
                      gates_ref, x_ref, p_ref, lng_ref, lnb_ref, sw_ref, sb_ref, pw_ref, ps_ref,
                      wa_ref, wb_ref, wc_ref, wd_ref, wo_ref, wple_ref, pleg_ref, wpg_ref, o_ref,
                      ya_sc, yb_sc, *, tm, seq):
    def gated(n, y, w_ref):
        gate = _sigmoid(gates_ref[:, n * D_MODEL:(n + 1) * D_MODEL])
        return gate.astype(F32) * jnp.dot(y, w_ref[...], preferred_element_type=F32)

    yc = _silu_gated(oc_ref[...], cz_ref[...])
    yd = _silu_gated(od_ref[...], dz_ref[...])
    merged = gated(2, yc, wc_ref) + gated(3, yd, wd_ref)
    e = jnp.dot(p_ref[...].astype(BF16), wple_ref[...], preferred_element_type=F32)

    _local_kernel(au_ref, av_ref, az_ref, bx_ref, halo_ref, bz_ref, lng_ref, lnb_ref, sw_ref, sb_ref,
                  pw_ref, ps_ref, ya_sc, yb_sc, tm=tm, seq=seq)

    merged = merged + gated(0, ya_sc[...], wa_ref) + gated(1, yb_sc[...], wb_ref)
    x1 = x_ref[...] + jnp.dot(merged.astype(BF16), wo_ref[...], preferred_element_type=F32)
    n1 = (x1 * lax.rsqrt(jnp.mean(x1 * x1, axis=-1, keepdims=True) + EPS) * pleg_ref[...]).astype(BF16)
    o_ref[...] = x1 + _sigmoid(jnp.dot(n1, wpg_ref[...], preferred_element_type=F32)) * e


def _mix_merge(oc, od, proj, x2d, p2d, seq, layer, lng, lnb, sgu_w, sgu_bt, pool_w, pool_scale,
               wa, wb, wc, wd, wo, wple, pleg, wpg):
    t, d = x2d.shape
    tm = 512
    w = BRANCH_WIDTH
    row = lambda width: pl.BlockSpec((tm, width), lambda i: (i, 0))
    col = lambda off: pl.BlockSpec((tm, w), lambda i: (i, off // w))
    halo = pl.BlockSpec((POOL_HALO, w),
                        lambda i: (jnp.maximum(i * (tm // POOL_HALO) - 1, 0), OFF_BX // w))
    params = (lng, lnb, sgu_w, sgu_bt, pool_w, pool_scale, wa, wb, wc, wd, wo, wple, pleg, wpg)
    return pl.pallas_call(
        functools.partial(_mix_merge_kernel, tm=tm, seq=seq),
        grid=(t // tm,),
        in_specs=[col(OFF_AU), col(OFF_AV), col(OFF_AZ), col(OFF_BX), halo, col(OFF_BZ),
                  row(w), row(w), col(OFF_CZ), col(OFF_DZ),
                  pl.BlockSpec((tm, 4 * d), lambda i: (i, OFF_GATES // (4 * d))),
                  row(d),
                  pl.BlockSpec((tm, PLE_DIM), lambda i: (layer * (t // tm) + i, 0))]
        + [_layer_spec(a, layer, single_buffer=True) for a in params],
        out_specs=row(d),
        out_shape=jax.ShapeDtypeStruct((t, d), F32),
        scratch_shapes=[pltpu.VMEM((tm, w), BF16), pltpu.VMEM((tm, w), BF16)],
        compiler_params=_params(("parallel",)),
        name="mix_merge",
    )(proj, proj, proj, proj, proj, proj, oc, od, proj, proj, proj, x2d, p2d, *params)


def kernel(x, p, positions, norm_g, w_in, ln_v_g, ln_v_b, sgu_w, sgu_b, w_a_out, pool_w, pool_scale,
           w_b_out, cq_norm_g, w_uq, ckv_norm_g, w_ukv, q_norm_g, k_norm_g, w_c_out, w_d_out, w_o,
           w_ple, ple_norm_g, w_ple_gate):
    batch, seq, d = x.shape
    depth = w_in.shape[0]
    t = batch * seq
    x2d = x.reshape(t, d)
    p2d = p.reshape(depth * t, PLE_DIM)
    assert d == D_MODEL and seq % 1024 == 0 and t % 2048 == 0, (batch, seq, d)
    cos_l, sin_l = _rope_tables(positions)

    row = lambda g: g.reshape(depth, 1, -1)
    bf16 = lambda w: w.astype(BF16)
    w_in_p = _stage_w_in(w_in)
    w_uq_p = _take_columns(bf16(w_uq), _uq_segments())
    qk_gain = lambda g: jnp.concatenate([g, g[:, MLA_NOPE:]], axis=1).reshape(depth, 1, MLA_QK_PAD)

    for i in range(depth):
        proj = _inproj(x2d, row(norm_g), w_in_p, i)
        q, k, v = _mla_prep(proj, cos_l, sin_l, i, row(cq_norm_g), w_uq_p, row(ckv_norm_g), bf16(w_ukv),
                            qk_gain(q_norm_g), qk_gain(k_norm_g))
        oc = _mla_attn(q, k, v, batch, seq)
        od = _sb_attn(proj, batch, seq)
        x2d = _mix_merge(oc, od, proj, x2d, p2d, seq, i, row(ln_v_g), row(ln_v_b), sgu_w,
                         jnp.swapaxes(sgu_b, 1, 2), bf16(pool_w), row(pool_scale), bf16(w_a_out),
                         bf16(w_b_out), bf16(w_c_out), bf16(w_d_out), bf16(w_o), bf16(w_ple),
                         row(ple_norm_g), bf16(w_ple_gate))
    return x2d.reshape(batch, seq, d)
```
